```python
import jax, jax.numpy as jnp
from jax import lax
import numpy as np

D_MODEL = 2048
BATCH = 2
SEQ = 16384
DEPTH = 1

GRID_W = 64
CTX_LEN = 256
NA_HEADS = 16
NA_HEAD_DIM = 64
NA_WIDTH = NA_HEADS * NA_HEAD_DIM
WIN_H = 8
WIN_W = 16
GLA_HEADS = 4
GLA_KEY_DIM = D_MODEL // 2
GLA_VAL_DIM = D_MODEL
GLA_HEAD_K = GLA_KEY_DIM // GLA_HEADS
GLA_HEAD_V = GLA_VAL_DIM // GLA_HEADS
GATE_RANK = 16
GATE_NORMALIZER = 16.0
GLA_CHUNK = 64
N_DIRS = 2
ROPE_BASE = 10000.0
ROPE_PAIRS = GLA_HEAD_K // 4
GATE_WIDTH = GLA_HEADS * 2 * ROPE_PAIRS
COLS = (NA_WIDTH, NA_WIDTH, NA_WIDTH, NA_WIDTH,
        GLA_KEY_DIM, GLA_KEY_DIM, GLA_VAL_DIM, GLA_VAL_DIM, N_DIRS * GATE_RANK,
        D_MODEL, D_MODEL)
PROJ_WIDTH = sum(COLS)
DEEPNORM_ALPHA = (2 * DEPTH) ** 0.25
DEEPNORM_BETA = (8 * DEPTH) ** -0.25
LN_EPS = 1e-6
RMS_EPS = 1e-6

kernel_name = "hybrid_natten_gla_gated_merge_block"


def _layer_norm(x, g=None, b=None):
    xf = x.astype(jnp.float32)
    mu = jnp.mean(xf, axis=-1, keepdims=True)
    var = jnp.mean(jnp.square(xf - mu), axis=-1, keepdims=True)
    y = (xf - mu) * lax.rsqrt(var + LN_EPS)
    if g is not None:
        y = y * g.astype(jnp.float32) + b.astype(jnp.float32)
    return y.astype(x.dtype)


def _split_cols(p):
    idx, acc = [], 0
    for w in COLS[:-1]:
        acc += w
        idx.append(acc)
    return jnp.split(p, idx, axis=-1)


def _heads(t, h):
    return t.reshape(t.shape[:-1] + (h, t.shape[-1] // h))


def _neighborhood_attention(q, k, v, k_ctx, v_ctx, rpb):
    bsz, s, h, d = q.shape
    rows = s // GRID_W
    kh = min(WIN_H, rows)
    qg = q.reshape(bsz, rows, GRID_W, h, d) * (d ** -0.5)
    kg = k.reshape(bsz, rows, GRID_W, h, d)
    vg = v.reshape(bsz, rows, GRID_W, h, d)
    cols = jnp.arange(GRID_W)
    c_start = jnp.clip(cols - WIN_W // 2, 0, GRID_W - WIN_W)
    col_idx = c_start[:, None] + jnp.arange(WIN_W)[None, :]
    dc = col_idx - cols[:, None] + (WIN_W - 1)
    n_loc = kh * WIN_W

    def row_block(r):
        r_start = jnp.clip(r - kh // 2, 0, rows - kh)
        q_r = lax.dynamic_index_in_dim(qg, r, axis=1, keepdims=False)
        k_rows = lax.dynamic_slice_in_dim(kg, r_start, kh, axis=1)
        v_rows = lax.dynamic_slice_in_dim(vg, r_start, kh, axis=1)
        k_win = k_rows[:, :, col_idx]
        v_win = v_rows[:, :, col_idx]
        dr = r_start + jnp.arange(kh) - r + (WIN_H - 1)
        bias = rpb[:, dr[:, None, None], dc[None]]
        s_loc = jnp.einsum('bqhd,biqjhd->bhqij', q_r, k_win) + bias.transpose(0, 2, 1, 3)[None]
        s_ctx = jnp.einsum('bqhd,bkhd->bhqk', q_r, k_ctx)
        scores = jnp.concatenate([s_loc.reshape(bsz, h, GRID_W, n_loc), s_ctx], axis=-1)
        p = jax.nn.softmax(scores.astype(jnp.float32), axis=-1).astype(v.dtype)
        p_loc = p[..., :n_loc].reshape(bsz, h, GRID_W, kh, WIN_W)
        p_ctx = p[..., n_loc:]
        return (jnp.einsum('bhqij,biqjhd->bqhd', p_loc, v_win)
                + jnp.einsum('bhqk,bkhd->bqhd', p_ctx, v_ctx))

    o = lax.map(row_block, jnp.arange(rows))
    return o.transpose(1, 0, 2, 3, 4).reshape(bsz, s, h * d)


def _context_attention(q, k, v):
    d = q.shape[-1]
    scores = jnp.einsum('bqhd,bkhd->bhqk', q * (d ** -0.5), k)
    p = jax.nn.softmax(scores.astype(jnp.float32), axis=-1).astype(v.dtype)
    o = jnp.einsum('bhqk,bkhd->bqhd', p, v)
    return o.reshape(o.shape[:2] + (-1,))


def _axial_rope(x, row, col):
    inv_freq = ROPE_BASE ** (-jnp.arange(ROPE_PAIRS, dtype=jnp.float32) / ROPE_PAIRS)

    def rot(u, pos):
        ang = pos[:, None] * inv_freq[None, :]
        cos = jnp.cos(ang)[None, :, None, :]
        sin = jnp.sin(ang)[None, :, None, :]
        u1, u2 = u[..., :ROPE_PAIRS], u[..., ROPE_PAIRS:]
        return jnp.concatenate([u1 * cos - u2 * sin, u1 * sin + u2 * cos], axis=-1)

    half = x.shape[-1] // 2
    return jnp.concatenate([rot(x[..., :half], row), rot(x[..., half:], col)], axis=-1)


def _gla_log_decay(bg, w_gate2, b_gate):
    lr = bg.reshape(bg.shape[:-1] + (N_DIRS, GATE_RANK)).astype(jnp.float32)
    g = jnp.einsum('btdr,drk->btdk', lr, w_gate2.astype(jnp.float32)) + b_gate.astype(jnp.float32)
    g = jax.nn.log_sigmoid(g) / GATE_NORMALIZER
    g = g.reshape(g.shape[:3] + (GLA_HEADS, 2, ROPE_PAIRS))
    g = jnp.concatenate([g, g], axis=-1).reshape(g.shape[:3] + (GLA_HEADS, GLA_HEAD_K))
    return g[:, :, 0], g[:, :, 1]


def _gla_chunked(q, k, v, g, s0):
    bsz, t, h, _ = q.shape
    dv = v.shape[-1]
    n = t // GLA_CHUNK

    def chunks(a):
        return a.reshape(bsz, n, GLA_CHUNK, h, a.shape[-1]).transpose(1, 0, 3, 2, 4)

    tril = jnp.tril(jnp.ones((GLA_CHUNK, GLA_CHUNK), jnp.float32))

    def step(s, inp):
        qc, kc, vc, gc = inp
        cum = jnp.cumsum(gc, axis=2)
        last = cum[:, :, -1:]
        q_e = qc * jnp.exp(cum)
        k_e = kc * jnp.exp(-cum)
        att = jnp.einsum('bhik,bhjk->bhij', q_e, k_e) * tril
        o = jnp.einsum('bhij,bhjv->bhiv', att, vc) + jnp.einsum('bhik,bhkv->bhiv', q_e, s)
        s = (jnp.exp(last[:, :, 0])[..., None] * s
             + jnp.einsum('bhjk,bhjv->bhkv', kc * jnp.exp(last - cum), vc))
        return s, o

    s, o = lax.scan(step, s0, (chunks(q), chunks(k), chunks(v), chunks(g)))
    return o.transpose(1, 0, 3, 2, 4).reshape(bsz, t, h, dv), s


def _head_rms(o, g):
    y = o * lax.rsqrt(jnp.mean(jnp.square(o), axis=-1, keepdims=True) + RMS_EPS) * g.astype(jnp.float32)
    return y.reshape(y.shape[:2] + (-1,))


def _gla_mixer(bq, bk, bv, bg, bq_c, bk_c, bv_c, bg_c, w_gate2, b_gate, norm_g, need_ctx_out):
    f32 = jnp.float32
    t = bq.shape[1]
    pos = jnp.arange(t)
    row = (pos // GRID_W).astype(f32)
    col = (pos % GRID_W).astype(f32)
    scale = GLA_HEAD_K ** -0.5
    q = _axial_rope(_heads(bq.astype(f32), GLA_HEADS) * scale, row, col)
    k = _axial_rope(_heads(bk.astype(f32), GLA_HEADS), row, col)
    v = _heads(bv.astype(f32), GLA_HEADS)
    g_f, g_b = _gla_log_decay(bg, w_gate2, b_gate)
    qc = _heads(bq_c.astype(f32), GLA_HEADS) * scale
    kc = _heads(bk_c.astype(f32), GLA_HEADS)
    vc = _heads(bv_c.astype(f32), GLA_HEADS)
    gc_f, gc_b = _gla_log_decay(bg_c, w_gate2, b_gate)
    s0 = jnp.zeros((q.shape[0], GLA_HEADS, GLA_HEAD_K, GLA_HEAD_V), f32)

    def flip(a):
        return a[:, ::-1]

    oc_f, s_f = _gla_chunked(qc, kc, vc, gc_f, s0)
    oc_b, s_b = _gla_chunked(flip(qc), flip(kc), flip(vc), flip(gc_b), s0)
    o_f, _ = _gla_chunked(q, k, v, g_f, s_f)
    o_b, _ = _gla_chunked(flip(q), flip(k), flip(v), flip(g_b), s_b)
    y = _head_rms(o_f + flip(o_b), norm_g).astype(bv.dtype)
    y_c = _head_rms(oc_f + flip(oc_b), norm_g).astype(bv.dtype) if need_ctx_out else None
    return y, y_c


def _merge_branches(y_a, az, y_b, bz, mga, mgb, w_br_a, w_br_b, w_out):
    p_a = (y_a * jax.nn.silu(az)) @ w_br_a
    p_b = (y_b * jax.nn.silu(bz)) @ w_br_b
    return (jax.nn.sigmoid(mga) * p_a + jax.nn.sigmoid(mgb) * p_b) @ w_out


def setup_inputs(seed: int = 0) -> dict:
    key = jax.random.key(seed)
    ks = jax.random.split(key, 16)
    nrm = jax.random.normal
    f32 = jnp.float32
    return {
        "x": nrm(ks[0], (BATCH, SEQ, D_MODEL), f32),
        "c": nrm(ks[1], (BATCH, D_MODEL), f32),
        "ctx": nrm(ks[2], (BATCH, CTX_LEN, D_MODEL), f32),
        "c_ctx": nrm(ks[3], (D_MODEL,), f32),
        "w_mod": nrm(ks[4], (DEPTH, D_MODEL, 3 * D_MODEL), f32) * D_MODEL ** -0.5,
        "b_mod": 0.02 * nrm(ks[5], (DEPTH, 3 * D_MODEL), f32),
        "w_in": nrm(ks[6], (DEPTH, D_MODEL, PROJ_WIDTH), f32) * D_MODEL ** -0.5,
        "na_rpb": 0.1 * nrm(ks[7], (DEPTH, NA_HEADS, 2 * WIN_H - 1, 2 * WIN_W - 1), f32),
        "gla_w_gate2": nrm(ks[8], (DEPTH, N_DIRS, GATE_RANK, GATE_WIDTH), f32) * GATE_RANK ** -0.5,
        "gla_b_gate": 0.1 * nrm(ks[9], (DEPTH, N_DIRS, GATE_WIDTH), f32),
        "gla_norm_g": 1.0 + 0.02 * nrm(ks[10], (DEPTH, GLA_HEAD_V), f32),
        "w_br_a": nrm(ks[11], (DEPTH, NA_WIDTH, D_MODEL), f32) * NA_WIDTH ** -0.5 * DEEPNORM_BETA,
        "w_br_b": nrm(ks[12], (DEPTH, GLA_VAL_DIM, D_MODEL), f32) * GLA_VAL_DIM ** -0.5 * DEEPNORM_BETA,
        "w_out": nrm(ks[13], (DEPTH, D_MODEL, D_MODEL), f32) * D_MODEL ** -0.5 * DEEPNORM_BETA,
        "ln_g": 1.0 + 0.02 * nrm(ks[14], (DEPTH, D_MODEL), f32),
        "ln_b": 0.02 * nrm(ks[15], (DEPTH, D_MODEL), f32),
    }


def reference(x, c, ctx, c_ctx, w_mod, b_mod, w_in, na_rpb, gla_w_gate2, gla_b_gate,
              gla_norm_g, w_br_a, w_br_b, w_out, ln_g, ln_b):
    for l in range(DEPTH):
        update_ctx = l < DEPTH - 1
        mod = jax.nn.silu(c) @ w_mod[l] + b_mod[l]
        shift, scale, gate = jnp.split(mod[:, None, :], 3, axis=-1)
        mod_c = jax.nn.silu(c_ctx) @ w_mod[l] + b_mod[l]
        shift_c, scale_c, gate_c = jnp.split(mod_c, 3, axis=-1)
        h = _layer_norm(x) * (1.0 + scale) + shift
        h_c = _layer_norm(ctx) * (1.0 + scale_c) + shift_c
        aq, ak, av, az, bq, bk, bv, bz, bg, mga, mgb = _split_cols(h @ w_in[l])
        aq_c, ak_c, av_c, az_c, bq_c, bk_c, bv_c, bz_c, bg_c, mga_c, mgb_c = _split_cols(h_c @ w_in[l])
        k_ctx = _heads(ak_c, NA_HEADS)
        v_ctx = _heads(av_c, NA_HEADS)
        y_a = _neighborhood_attention(_heads(aq, NA_HEADS), _heads(ak, NA_HEADS), _heads(av, NA_HEADS),
                                      k_ctx, v_ctx, na_rpb[l])
        y_b, y_b_c = _gla_mixer(bq, bk, bv, bg, bq_c, bk_c, bv_c, bg_c,
                                gla_w_gate2[l], gla_b_gate[l], gla_norm_g[l], update_ctx)
        out = _merge_branches(y_a, az, y_b, bz, mga, mgb, w_br_a[l], w_br_b[l], w_out[l])
        x_new = _layer_norm(DEEPNORM_ALPHA * x + gate * out, ln_g[l], ln_b[l])
        if update_ctx:
            y_a_c = _context_attention(_heads(aq_c, NA_HEADS), k_ctx, v_ctx)
            out_c = _merge_branches(y_a_c, az_c, y_b_c, bz_c, mga_c, mgb_c, w_br_a[l], w_br_b[l], w_out[l])
            ctx = _layer_norm(DEEPNORM_ALPHA * ctx + gate_c * out_c, ln_g[l], ln_b[l])
        x = x_new
    return x
```

```python
import functools

import jax
import jax.numpy as jnp
from jax import lax
from jax.experimental import pallas as pl
from jax.experimental.pallas import tpu as pltpu

F32 = jnp.float32
BF16 = jnp.bfloat16

GRID_W = 64
WIN_H = 8
WIN_W = 16
NA_HEADS = 16
NA_HEAD_DIM = 64
NA_WIDTH = NA_HEADS * NA_HEAD_DIM
GLA_HEADS = 4
GATE_RANK = 16
GATE_NORMALIZER = 16.0
GLA_CHUNK = 64
N_DIRS = 2
ROPE_BASE = 10000.0
DEPTH = 1
DEEPNORM_ALPHA = (2 * DEPTH) ** 0.25
LN_EPS = 1e-6
RMS_EPS = 1e-6
MASK_VALUE = -1e30

LANES = 128
MOD_ROWS = 8
VMEM_LIMIT = 56 * 2 ** 20


def _params(sem, vmem=VMEM_LIMIT):
    return pltpu.CompilerParams(dimension_semantics=sem, vmem_limit_bytes=vmem)


def _mod_kernel(c_ref, w_ref, b_ref, o_ref):
    c = c_ref[...]
    s = c * jax.nn.sigmoid(c)
    o_ref[...] = jnp.dot(s, w_ref[...], preferred_element_type=F32) + b_ref[...]


def _modulation(cs, w, b, tn=768):
    rows, d = cs.shape
    n = w.shape[1]
    return pl.pallas_call(
        _mod_kernel,
        grid=(n // tn,),
        in_specs=[pl.BlockSpec((rows, d), lambda j: (0, 0)),
                  pl.BlockSpec((d, tn), lambda j: (0, j)),
                  pl.BlockSpec((1, tn), lambda j: (0, j))],
        out_specs=pl.BlockSpec((rows, tn), lambda j: (0, j)),
        out_shape=jax.ShapeDtypeStruct((rows, n), F32),
        compiler_params=_params(("parallel",)),
        name="mod",
    )(cs, w, b)


def _any_of(n, idxs):
    return functools.reduce(jnp.logical_or, [n == i for i in idxs])


def _proj_kernel(x_ref, mod_ref, w_ref, wg_ref, cos_ref, sin_ref, p_ref, g_ref, h_ref, *, kinds, ln_rows):
    n = pl.program_id(2)
    tm = x_ref.shape[1]
    tn = w_ref.shape[1]

    @pl.when(n == 0)
    def _():
        shift = mod_ref[0, 0:1, :]
        scale1 = 1.0 + mod_ref[0, 1:2, :]
        for r0 in range(0, tm, ln_rows):
            xs = x_ref[0, r0:r0 + ln_rows, :]
            mu = jnp.mean(xs, axis=-1, keepdims=True)
            xc = xs - mu
            var = jnp.mean(xc * xc, axis=-1, keepdims=True)
            hb = (xc * lax.rsqrt(var + LN_EPS) * scale1 + shift).astype(BF16)
            h_ref[r0:r0 + ln_rows, :] = hb
            g_ref[0, r0:r0 + ln_rows, :] = jnp.dot(hb, wg_ref[...], preferred_element_type=F32)

    acc = jnp.dot(h_ref[...], w_ref[...], preferred_element_type=F32)

    def rope(a):
        outs = []
        for j in range(tn // LANES):
            u = a[:, j * LANES:(j + 1) * LANES]
            t = (j % 2) * LANES
            outs.append(u * cos_ref[:, t:t + LANES] + pltpu.roll(u, LANES // 2, 1) * sin_ref[:, t:t + LANES])
        return jnp.concatenate(outs, axis=1)

    epilogues = {
        "plain": lambda a: a,
        "na_q": lambda a: a * (NA_HEAD_DIM ** -0.5),
        "gla_q": lambda a: rope(a * (w_ref.shape[0] // 2 // GLA_HEADS) ** -0.5),
        "gla_k": rope,
        "silu": lambda a: a * jax.nn.sigmoid(a),
        "sigmoid": jax.nn.sigmoid,
    }
    for kind, fn in epilogues.items():
        idxs = [i for i, k in enumerate(kinds) if k == kind]
        if not idxs:
            continue

        @pl.when(_any_of(n, idxs))
        def _(fn=fn):
            p_ref[0] = fn(acc).astype(p_ref.dtype)


def _projection(xin, mod, w, wg, cos_tab, sin_tab, kinds, tm, tn, ln_rows=256):
    b, t, d = xin.shape
    n = w.shape[1]
    ln_rows = min(ln_rows, tm)
    kern = functools.partial(_proj_kernel, kinds=tuple(kinds), ln_rows=ln_rows)
    return pl.pallas_call(
        kern,
        grid=(b, t // tm, n // tn),
        in_specs=[pl.BlockSpec((1, tm, d), lambda bi, i, j: (bi, i, 0)),
                  pl.BlockSpec((1, 3, d), lambda bi, i, j: (bi, 0, 0)),
                  pl.BlockSpec((d, tn), lambda bi, i, j: (0, j)),
                  pl.BlockSpec((d, LANES), lambda bi, i, j: (0, 0)),
                  pl.BlockSpec((tm, 2 * LANES), lambda bi, i, j: (i, 0)),
                  pl.BlockSpec((tm, 2 * LANES), lambda bi, i, j: (i, 0))],
        out_specs=[pl.BlockSpec((1, tm, tn), lambda bi, i, j: (bi, i, j)),
                   pl.BlockSpec((1, tm, LANES), lambda bi, i, j: (bi, i, 0))],
        out_shape=[jax.ShapeDtypeStruct((b, t, n), BF16),
                   jax.ShapeDtypeStruct((b, t, LANES), F32)],
        scratch_shapes=[pltpu.VMEM((tm, d), BF16)],
        compiler_params=_params(("parallel", "parallel", "arbitrary")),
        name="proj",
    )(xin, mod, w, wg, cos_tab, sin_tab)


def _na_kernel(q_ref, k_ref, v_ref, kc_ref, vc_ref, bias_ref, o_ref, *, rb, rows):
    blk = pl.program_id(2)
    hd = NA_HEAD_DIM
    is_a = lax.broadcasted_iota(jnp.int32, (GRID_W, 2 * hd), 1) < hd
    kc = kc_ref[0]
    vc = vc_ref[0]
    contract_last = (((1,), (1,)), ((), ()))

    def body(i, carry):
        r = blk * rb + i
        r_start = jnp.clip(r - WIN_H // 2, 0, rows - WIN_H)
        cls = r_start - r + (WIN_H - 1)
        q = q_ref[0, pl.ds(pl.multiple_of(i * GRID_W, GRID_W), GRID_W), :]
        zero = jnp.zeros_like(q)
        qs = jnp.concatenate([jnp.where(is_a, q, zero), jnp.where(is_a, zero, q)], axis=0)
        k0 = pl.multiple_of(r_start * GRID_W, GRID_W)
        k8 = k_ref[0, pl.ds(k0, WIN_H * GRID_W), :]
        v8 = v_ref[0, pl.ds(k0, WIN_H * GRID_W), :]
        s_loc = lax.dot_general(qs, k8, contract_last, preferred_element_type=F32) + bias_ref[cls, 0]
        s_ctx = lax.dot_general(qs, kc, contract_last, preferred_element_type=F32)
        m = jnp.maximum(jnp.max(s_loc, axis=-1, keepdims=True), jnp.max(s_ctx, axis=-1, keepdims=True))
        p_loc = jnp.exp(s_loc - m)
        p_ctx = jnp.exp(s_ctx - m)
        l = jnp.sum(p_loc, axis=-1, keepdims=True) + jnp.sum(p_ctx, axis=-1, keepdims=True)
        o = (jnp.dot(p_loc.astype(BF16), v8, preferred_element_type=F32)
             + jnp.dot(p_ctx.astype(BF16), vc, preferred_element_type=F32))
        o = o / l
        out = jnp.where(is_a, o[:GRID_W], o[GRID_W:])
        o_ref[0, pl.ds(pl.multiple_of(i * GRID_W, GRID_W), GRID_W), :] = out.astype(o_ref.dtype)
        return carry

    lax.fori_loop(0, rb, body, 0)


def _neighbourhood_attention(p, p_c, bias, rb=8):
    b, s, _ = p.shape
    lc = p_c.shape[1]
    rows = s // GRID_W
    rb = min(rb, rows)
    hg = NA_WIDTH // LANES
    kern = functools.partial(_na_kernel, rb=rb, rows=rows)
    return pl.pallas_call(
        kern,
        grid=(b, hg, rows // rb),
        in_specs=[pl.BlockSpec((1, rb * GRID_W, LANES), lambda bi, g, i: (bi, i, g)),
                  pl.BlockSpec((1, s, LANES), lambda bi, g, i: (bi, 0, hg + g)),
                  pl.BlockSpec((1, s, LANES), lambda bi, g, i: (bi, 0, 2 * hg + g)),
                  pl.BlockSpec((1, lc, LANES), lambda bi, g, i: (bi, 0, hg + g)),
                  pl.BlockSpec((1, lc, LANES), lambda bi, g, i: (bi, 0, 2 * hg + g)),
                  pl.BlockSpec((WIN_H, 1, 2 * GRID_W, WIN_H * GRID_W), lambda bi, g, i: (0, g, 0, 0))],
        out_specs=pl.BlockSpec((1, rb * GRID_W, LANES), lambda bi, g, i: (bi, i, g)),
        out_shape=jax.ShapeDtypeStruct((b, s, NA_WIDTH), BF16),
        compiler_params=_params(("parallel", "parallel", "arbitrary")),
        name="na",
    )(p, p, p, p_c, p_c, bias)


def _na_bias_table(rpb):
    qcol = jnp.arange(GRID_W)[:, None]
    kcol = jnp.arange(GRID_W)[None, :]
    c_start = jnp.clip(qcol - WIN_W // 2, 0, GRID_W - WIN_W)
    in_win = (kcol >= c_start) & (kcol < c_start + WIN_W)
    dc = jnp.clip(kcol - qcol + (WIN_W - 1), 0, 2 * WIN_W - 2)
    band = jnp.where(in_win[None, None], rpb[:, :, dc], MASK_VALUE)
    dr = jnp.arange(WIN_H)[:, None] + jnp.arange(WIN_H)[None, :]
    t = band[:, dr]
    t = t.transpose(1, 0, 3, 2, 4)
    return t.reshape(WIN_H, NA_HEADS // 2, 2 * GRID_W, WIN_H * GRID_W).astype(F32)


def _gla_kernel(*refs, reverse, nsub, has_init, emit_o, combine, emit_state):
    refs = list(refs)
    q_ref, k_ref, v_ref, lr_ref, w2_ref, b2_ref = refs[:6]
    pos = 6
    s0_ref = ob_ref = ng_ref = out_ref = sfin_ref = None
    if has_init:
        s0_ref = refs[pos]; pos += 1
    if combine:
        ob_ref, ng_ref = refs[pos], refs[pos + 1]; pos += 2
    if emit_o:
        out_ref = refs[pos]; pos += 1
    if emit_state:
        sfin_ref = refs[pos]; pos += 1
    s_ref = refs[pos]

    j = pl.program_id(2)
    nblk = pl.num_programs(2)
    c_len = GLA_CHUNK
    dk = q_ref.shape[2]
    dv = v_ref.shape[2]

    @pl.when(j == 0)
    def _():
        if has_init:
            s_ref[...] = s0_ref[0, 0]
        else:
            s_ref[...] = jnp.zeros_like(s_ref)

    z = jnp.dot(lr_ref[0], w2_ref[0], preferred_element_type=F32) + b2_ref[0]
    g = (jnp.minimum(z, 0.0) - jnp.log1p(jnp.exp(-jnp.abs(z)))) * (1.0 / GATE_NORMALIZER)

    ri = lax.broadcasted_iota(jnp.int32, (c_len, c_len), 0)
    ci = lax.broadcasted_iota(jnp.int32, (c_len, c_len), 1)
    tri = (ci >= ri) if reverse else (ci <= ri)
    tri_f = tri.astype(F32)
    ones_cols = jnp.ones((c_len, LANES), F32)
    contract_last = (((1,), (1,)), ((), ()))
    contract_first = (((0,), (0,)), ((), ()))

    order = range(nsub - 1, -1, -1) if reverse else range(nsub)
    for c in order:
        sl = slice(c * c_len, (c + 1) * c_len)
        gc = g[sl]
        cum = jnp.dot(tri_f, gc, preferred_element_type=F32)
        last = jnp.sum(gc, axis=0, keepdims=True)
        qc = q_ref[0, sl, :].astype(F32)
        kc = k_ref[0, sl, :].astype(F32)
        vc = v_ref[0, sl, :]
        s_prev = s_ref[...]
        qe = (qc * jnp.exp(cum)).astype(BF16)
        if emit_o:
            ke = (kc * jnp.exp(-cum)).astype(BF16)
            att = lax.dot_general(qe, ke, contract_last, preferred_element_type=F32)
            att = jnp.where(tri, att, 0.0).astype(BF16)
            o = (jnp.dot(att, vc, preferred_element_type=F32)
                 + jnp.dot(qe, s_prev.astype(BF16), preferred_element_type=F32))
            if combine:
                tot = o + ob_ref[0, sl, :].astype(F32)
                ms = jnp.mean(tot * tot, axis=-1, keepdims=True)
                o = tot * lax.rsqrt(ms + RMS_EPS) * ng_ref[...]
            out_ref[0, sl, :] = o.astype(out_ref.dtype)
        kl = (kc * jnp.exp(last - cum)).astype(BF16)
        kv = lax.dot_general(kl, vc, contract_first, preferred_element_type=F32)
        dcol = lax.dot_general(gc, ones_cols, contract_first, preferred_element_type=F32)
        dec = jnp.exp(dcol)
        s_ref[...] = jnp.concatenate([dec] * (dv // LANES), axis=1) * s_prev + kv

    if emit_state:
        @pl.when(j == nblk - 1)
        def _():
            sfin_ref[0, 0] = s_ref[...]


def _gla_scan(p, g_lr, w2e, b2e, *, direction, tb, q_blk, k_blk, v_blk, dk, dv,
              s0=None, ob=None, norm_g=None, emit_o=True, emit_state=False):
    b, t, _ = p.shape
    nblk = t // tb
    reverse = direction == 1
    combine = ob is not None
    has_init = s0 is not None

    def blk(j):
        return (nblk - 1 - j) if reverse else j

    in_specs = [pl.BlockSpec((1, tb, dk), lambda bi, h, j: (bi, blk(j), q_blk + h)),
                pl.BlockSpec((1, tb, dk), lambda bi, h, j: (bi, blk(j), k_blk + h)),
                pl.BlockSpec((1, tb, dv), lambda bi, h, j: (bi, blk(j), v_blk + h)),
                pl.BlockSpec((1, tb, LANES), lambda bi, h, j: (bi, blk(j), 0)),
                pl.BlockSpec((1, LANES, dk), lambda bi, h, j: (direction, 0, h)),
                pl.BlockSpec((1, 1, dk), lambda bi, h, j: (direction, 0, h))]
    args = [p, p, p, g_lr, w2e, b2e]
    if has_init:
        in_specs.append(pl.BlockSpec((1, 1, dk, dv), lambda bi, h, j: (bi, h, 0, 0)))
        args.append(s0)
    if combine:
        in_specs.append(pl.BlockSpec((1, tb, dv), lambda bi, h, j: (bi, blk(j), h)))
        in_specs.append(pl.BlockSpec((1, dv), lambda bi, h, j: (0, 0)))
        args += [ob, norm_g]
    out_specs, out_shape = [], []
    if emit_o:
        out_specs.append(pl.BlockSpec((1, tb, dv), lambda bi, h, j: (bi, blk(j), h)))
        out_shape.append(jax.ShapeDtypeStruct((b, t, GLA_HEADS * dv), BF16))
    if emit_state:
        out_specs.append(pl.BlockSpec((1, 1, dk, dv), lambda bi, h, j: (bi, h, 0, 0)))
        out_shape.append(jax.ShapeDtypeStruct((b, GLA_HEADS, dk, dv), F32))
    kern = functools.partial(_gla_kernel, reverse=reverse, nsub=tb // GLA_CHUNK, has_init=has_init,
                             emit_o=emit_o, combine=combine, emit_state=emit_state)
    res = pl.pallas_call(
        kern,
        grid=(b, GLA_HEADS, nblk),
        in_specs=in_specs,
        out_specs=out_specs,
        out_shape=out_shape,
        scratch_shapes=[pltpu.VMEM((dk, dv), F32)],
        compiler_params=_params(("parallel", "parallel", "arbitrary")),
        name="gla_" + ("bwd" if reverse else "fwd") + ("_ctx" if not emit_o else ""),
    )(*args)
    return res[0] if len(res) == 1 else res


def _merge_kernel(ya_ref, az_ref, yb_ref, bz_ref, ga_ref, gb_ref, x_ref, mod_ref,
                  wa_ref, wb_ref, wo_ref, lng_ref, lnb_ref, o_ref):
    ua = (ya_ref[0].astype(F32) * az_ref[0].astype(F32)).astype(BF16)
    ub = (yb_ref[0].astype(F32) * bz_ref[0].astype(F32)).astype(BF16)
    pa = jnp.dot(ua, wa_ref[...], preferred_element_type=F32)
    pb = jnp.dot(ub, wb_ref[...], preferred_element_type=F32)
    m = (ga_ref[0].astype(F32) * pa + gb_ref[0].astype(F32) * pb).astype(BF16)
    out = jnp.dot(m, wo_ref[...], preferred_element_type=F32)
    z = DEEPNORM_ALPHA * x_ref[0] + mod_ref[0, 2:3, :] * out
    mu = jnp.mean(z, axis=-1, keepdims=True)
    zc = z - mu
    var = jnp.mean(zc * zc, axis=-1, keepdims=True)
    o_ref[0] = zc * lax.rsqrt(var + LN_EPS) * lng_ref[...] + lnb_ref[...]


def _merge(y_a, y_b, p, x, mod, wa, wb, wo, ln_g, ln_b, az_blk, bz_blk, ga_blk, gb_blk, tm=256):
    b, s, d = x.shape
    na = y_a.shape[2]
    dv = y_b.shape[2]
    const = dict(pipeline_mode=pl.Buffered(1))
    return pl.pallas_call(
        _merge_kernel,
        grid=(b, s // tm),
        in_specs=[pl.BlockSpec((1, tm, na), lambda bi, i: (bi, i, 0)),
                  pl.BlockSpec((1, tm, na), lambda bi, i: (bi, i, az_blk)),
                  pl.BlockSpec((1, tm, dv), lambda bi, i: (bi, i, 0)),
                  pl.BlockSpec((1, tm, dv), lambda bi, i: (bi, i, bz_blk)),
                  pl.BlockSpec((1, tm, d), lambda bi, i: (bi, i, ga_blk)),
                  pl.BlockSpec((1, tm, d), lambda bi, i: (bi, i, gb_blk)),
                  pl.BlockSpec((1, tm, d), lambda bi, i: (bi, i, 0)),
                  pl.BlockSpec((1, 3, d), lambda bi, i: (bi, 0, 0)),
                  pl.BlockSpec((na, d), lambda bi, i: (0, 0), **const),
                  pl.BlockSpec((dv, d), lambda bi, i: (0, 0), **const),
                  pl.BlockSpec((d, d), lambda bi, i: (0, 0), **const),
                  pl.BlockSpec((1, d), lambda bi, i: (0, 0)),
                  pl.BlockSpec((1, d), lambda bi, i: (0, 0))],
        out_specs=pl.BlockSpec((1, tm, d), lambda bi, i: (bi, i, 0)),
        out_shape=jax.ShapeDtypeStruct((b, s, d), F32),
        compiler_params=_params(("parallel", "parallel")),
        name="merge",
    )(y_a, p, y_b, p, p, p, x, mod, wa, wb, wo, ln_g, ln_b)


def _rope_tables(s):
    pairs = LANES // 2
    pos = jnp.arange(s)
    row = (pos // GRID_W).astype(F32)
    col = (pos % GRID_W).astype(F32)
    inv_freq = ROPE_BASE ** (-jnp.arange(pairs, dtype=F32) / pairs)
    ar = row[:, None] * inv_freq[None, :]
    ac = col[:, None] * inv_freq[None, :]
    cos_tab = jnp.concatenate([jnp.cos(ar), jnp.cos(ar), jnp.cos(ac), jnp.cos(ac)], axis=-1)
    sin_tab = jnp.concatenate([-jnp.sin(ar), jnp.sin(ar), -jnp.sin(ac), jnp.sin(ac)], axis=-1)
    return cos_tab, sin_tab


def _expand_gate(w_gate2, b_gate, dk):
    dirs, rank, _ = w_gate2.shape
    pairs = dk // 4

    def dup(a):
        a = a.reshape(a.shape[:-1] + (GLA_HEADS, 2, pairs))
        a = jnp.concatenate([a, a], axis=-1)
        return a.reshape(a.shape[:-3] + (GLA_HEADS * dk,))

    w = dup(w_gate2.astype(F32))
    w2e = jnp.zeros((dirs, LANES, GLA_HEADS * dk), F32)
    for d in range(dirs):
        w2e = w2e.at[d, d * rank:(d + 1) * rank].set(w[d])
    b2e = dup(b_gate.astype(F32))[:, None, :]
    return w2e, b2e


def kernel(x, c, ctx, c_ctx, w_mod, b_mod, w_in, na_rpb, gla_w_gate2, gla_b_gate, gla_norm_g,
           w_br_a, w_br_b, w_out, ln_g, ln_b):
    bsz, s, d = x.shape
    lc = ctx.shape[1]
    key_dim, val_dim = d // 2, d
    dk, dv = key_dim // GLA_HEADS, val_dim // GLA_HEADS
    lyr = 0

    widths = (NA_WIDTH,) * 4 + (key_dim, key_dim, val_dim, val_dim, N_DIRS * GATE_RANK, d, d)
    offs = [0]
    for w_ in widths:
        offs.append(offs[-1] + w_)
    w_full = w_in[lyr]
    cols = [w_full[:, offs[i]:offs[i + 1]] for i in range(len(widths))]
    aq, ak, av, az, bq, bk, bv, bz, bg, mga, mgb = cols
    w_main = jnp.concatenate([aq, ak, av, az, bq, bk, bv, bz, mga, mgb], axis=1).astype(BF16)
    w_gate1 = jnp.pad(bg, ((0, 0), (0, LANES - bg.shape[1]))).astype(BF16)
    tn = 1024
    kinds = (["na_q"] * (NA_WIDTH // tn) + ["plain"] * (2 * NA_WIDTH // tn) + ["silu"] * (NA_WIDTH // tn)
             + ["gla_q"] * (key_dim // tn) + ["gla_k"] * (key_dim // tn) + ["plain"] * (val_dim // tn)
             + ["silu"] * (val_dim // tn) + ["sigmoid"] * (2 * d // tn))
    o_az = 3 * NA_WIDTH
    o_bq = 4 * NA_WIDTH
    o_bk = o_bq + key_dim
    o_bv = o_bk + key_dim
    o_bz = o_bv + val_dim
    o_ga = o_bz + val_dim
    o_gb = o_ga + d

    cs = jnp.concatenate([c, c_ctx[None], jnp.zeros((MOD_ROWS - bsz - 1, d), F32)], axis=0)
    mod = _modulation(cs, w_mod[lyr], b_mod[lyr][None])
    mod_lat = mod[:bsz].reshape(bsz, 3, d)
    mod_ctx = jnp.broadcast_to(mod[bsz].reshape(1, 3, d), (bsz, 3, d))

    cos_tab, sin_tab = _rope_tables(s)
    p, g_lr = _projection(x, mod_lat, w_main, w_gate1, cos_tab, sin_tab, kinds, tm=min(1024, s), tn=tn)
    ones_tab = jnp.ones((lc, 2 * LANES), F32)
    p_c, g_lr_c = _projection(ctx, mod_ctx, w_main, w_gate1, ones_tab, jnp.zeros_like(ones_tab), kinds,
                              tm=lc, tn=tn)

    y_a = _neighbourhood_attention(p, p_c, _na_bias_table(na_rpb[lyr]))

    w2e, b2e = _expand_gate(gla_w_gate2[lyr], gla_b_gate[lyr], dk)
    blocks = dict(q_blk=o_bq // dk, k_blk=o_bk // dk, v_blk=o_bv // dv, dk=dk, dv=dv)
    s_f = _gla_scan(p_c, g_lr_c, w2e, b2e, direction=0, tb=lc, emit_o=False, emit_state=True, **blocks)
    s_b = _gla_scan(p_c, g_lr_c, w2e, b2e, direction=1, tb=lc, emit_o=False, emit_state=True, **blocks)
    tb = min(256, s)
    o_b = _gla_scan(p, g_lr, w2e, b2e, direction=1, tb=tb, s0=s_b, **blocks)
    y_b = _gla_scan(p, g_lr, w2e, b2e, direction=0, tb=tb, s0=s_f, ob=o_b,
                    norm_g=gla_norm_g[lyr][None].astype(F32), **blocks)

    return _merge(y_a, y_b, p, x, mod_lat,
                  w_br_a[lyr].astype(BF16), w_br_b[lyr].astype(BF16), w_out[lyr].astype(BF16),
                  ln_g[lyr][None], ln_b[lyr][None],
                  az_blk=o_az // NA_WIDTH, bz_blk=o_bz // val_dim, ga_blk=o_ga // d, gb_blk=o_gb // d)
```

```python
import functools

import jax
import jax.numpy as jnp
from jax import lax
from jax.experimental import pallas as pl
from jax.experimental.pallas import tpu as pltpu

F32 = jnp.float32
BF16 = jnp.bfloat16

GRID_W = 64
WIN_H = 8
WIN_W = 16
NA_HEADS = 16
NA_HEAD_DIM = 64
NA_WIDTH = NA_HEADS * NA_HEAD_DIM
GLA_HEADS = 4
GATE_RANK = 16
GATE_NORMALIZER = 16.0
GLA_CHUNK = 64
GLA_BLOCK = 4 * GLA_CHUNK
N_DIRS = 2
ROPE_BASE = 10000.0
DEPTH = 1
DEEPNORM_ALPHA = (2 * DEPTH) ** 0.25
LN_EPS = 1e-6
RMS_EPS = 1e-6
MASK_VALUE = -1e30

LANES = 128
MOD_ROWS = 8
VMEM_LIMIT = 56 * 2 ** 20


def _params(sem, vmem=VMEM_LIMIT):
    return pltpu.CompilerParams(dimension_semantics=sem, vmem_limit_bytes=vmem)


def _mod_kernel(c_ref, w_ref, b_ref, o_ref):
    c = c_ref[...]
    s = c * jax.nn.sigmoid(c)
    o_ref[...] = jnp.dot(s, w_ref[...], preferred_element_type=F32) + b_ref[...]


def _modulation(cs, w, b, tn=768):
    rows, d = cs.shape
    n = w.shape[1]
    return pl.pallas_call(
        _mod_kernel,
        grid=(n // tn,),
        in_specs=[pl.BlockSpec((rows, d), lambda j: (0, 0)),
                  pl.BlockSpec((d, tn), lambda j: (0, j)),
                  pl.BlockSpec((1, tn), lambda j: (0, j))],
        out_specs=pl.BlockSpec((rows, tn), lambda j: (0, j)),
        out_shape=jax.ShapeDtypeStruct((rows, n), F32),
        compiler_params=_params(("parallel",)),
        name="mod",
    )(cs, w, b)


def _any_of(n, idxs):
    return functools.reduce(jnp.logical_or, [n == i for i in idxs])


def _proj_kernel(x_ref, mod_ref, w_ref, wg_ref, cos_ref, sin_ref, p_ref, g_ref, h_ref, *, kinds, ln_rows):
    n = pl.program_id(2)
    tm = x_ref.shape[1]
    tn = w_ref.shape[1]

    @pl.when(n == 0)
    def _():
        shift = mod_ref[0, 0:1, :]
        scale1 = 1.0 + mod_ref[0, 1:2, :]
        for r0 in range(0, tm, ln_rows):
            xs = x_ref[0, r0:r0 + ln_rows, :]
            mu = jnp.mean(xs, axis=-1, keepdims=True)
            xc = xs - mu
            var = jnp.mean(xc * xc, axis=-1, keepdims=True)
            hb = (xc * lax.rsqrt(var + LN_EPS) * scale1 + shift).astype(BF16)
            h_ref[r0:r0 + ln_rows, :] = hb
            g_ref[0, r0:r0 + ln_rows, :] = jnp.dot(hb, wg_ref[...], preferred_element_type=F32)

    acc = jnp.dot(h_ref[...], w_ref[...], preferred_element_type=F32)

    def rope(a):
        outs = []
        for j in range(tn // LANES):
            u = a[:, j * LANES:(j + 1) * LANES]
            t = (j % 2) * LANES
            outs.append(u * cos_ref[:, t:t + LANES] + pltpu.roll(u, LANES // 2, 1) * sin_ref[:, t:t + LANES])
        return jnp.concatenate(outs, axis=1)

    epilogues = {
        "plain": lambda a: a,
        "na_q": lambda a: a * (NA_HEAD_DIM ** -0.5),
        "gla_q": lambda a: rope(a * (w_ref.shape[0] // 2 // GLA_HEADS) ** -0.5),
        "gla_k": rope,
        "silu": lambda a: a * jax.nn.sigmoid(a),
        "sigmoid": jax.nn.sigmoid,
    }
    for kind, fn in epilogues.items():
        idxs = [i for i, k in enumerate(kinds) if k == kind]
        if not idxs:
            continue

        @pl.when(_any_of(n, idxs))
        def _(fn=fn):
            p_ref[0] = fn(acc).astype(p_ref.dtype)


def _projection(xin, mod, w, wg, cos_tab, sin_tab, kinds, tm, tn, ln_rows=256):
    b, t, d = xin.shape
    n = w.shape[1]
    ln_rows = min(ln_rows, tm)
    kern = functools.partial(_proj_kernel, kinds=tuple(kinds), ln_rows=ln_rows)
    return pl.pallas_call(
        kern,
        grid=(b, t // tm, n // tn),
        in_specs=[pl.BlockSpec((1, tm, d), lambda bi, i, j: (bi, i, 0)),
                  pl.BlockSpec((1, 3, d), lambda bi, i, j: (bi, 0, 0)),
                  pl.BlockSpec((d, tn), lambda bi, i, j: (0, j)),
                  pl.BlockSpec((d, LANES), lambda bi, i, j: (0, 0)),
                  pl.BlockSpec((tm, 2 * LANES), lambda bi, i, j: (i, 0)),
                  pl.BlockSpec((tm, 2 * LANES), lambda bi, i, j: (i, 0))],
        out_specs=[pl.BlockSpec((1, tm, tn), lambda bi, i, j: (bi, i, j)),
                   pl.BlockSpec((1, tm, LANES), lambda bi, i, j: (bi, i, 0))],
        out_shape=[jax.ShapeDtypeStruct((b, t, n), BF16),
                   jax.ShapeDtypeStruct((b, t, LANES), F32)],
        scratch_shapes=[pltpu.VMEM((tm, d), BF16)],
        compiler_params=_params(("parallel", "parallel", "arbitrary")),
        name="proj",
    )(xin, mod, w, wg, cos_tab, sin_tab)


def _na_kernel(q_ref, k_ref, v_ref, kc_ref, vc_ref, bias_ref, o_ref, s_buf, p_buf, l_buf, *, rb, rows):
    blk = pl.program_id(2)
    hd = NA_HEAD_DIM
    n_loc = WIN_H * GRID_W
    is_a = lax.broadcasted_iota(jnp.int32, (GRID_W, 2 * hd), 1) < hd
    contract_last = (((1,), (1,)), ((), ()))

    def window_start(i):
        r = blk * rb + i
        r_start = jnp.clip(r - WIN_H // 2, 0, rows - WIN_H)
        return r_start - r + (WIN_H - 1), pl.multiple_of(r_start * GRID_W, GRID_W)

    def scores(i):
        cls, k0 = window_start(i)
        q = q_ref[0, i * GRID_W:(i + 1) * GRID_W, :]
        zero = jnp.zeros_like(q)
        qs = jnp.concatenate([jnp.where(is_a, q, zero), jnp.where(is_a, zero, q)], axis=0)
        k8 = k_ref[0, pl.ds(k0, n_loc), :]
        s_buf[i % 2, :, :n_loc] = (lax.dot_general(qs, k8, contract_last, preferred_element_type=F32)
                                   + bias_ref[cls, 0])
        s_buf[i % 2, :, n_loc:] = lax.dot_general(qs, kc_ref[0], contract_last, preferred_element_type=F32)

    def softmax(i):
        s = s_buf[i % 2]
        m = jnp.max(s, axis=-1, keepdims=True)
        p = jnp.exp(s - m)
        l_buf[i % 2] = jnp.sum(p, axis=-1, keepdims=True)
        p_buf[i % 2] = p.astype(BF16)

    def values(i):
        _, k0 = window_start(i)
        v8 = v_ref[0, pl.ds(k0, n_loc), :]
        o = (jnp.dot(p_buf[i % 2, :, :n_loc], v8, preferred_element_type=F32)
             + jnp.dot(p_buf[i % 2, :, n_loc:], vc_ref[0], preferred_element_type=F32))
        o = o / l_buf[i % 2]
        out = jnp.where(is_a, o[:GRID_W], o[GRID_W:])
        o_ref[0, i * GRID_W:(i + 1) * GRID_W, :] = out.astype(o_ref.dtype)

    for t in range(rb + 2):
        if t >= 2:
            values(t - 2)
        if 1 <= t <= rb:
            softmax(t - 1)
        if t < rb:
            scores(t)


def _neighbourhood_attention(p, p_c, bias, rb=16):
    b, s, _ = p.shape
    lc = p_c.shape[1]
    rows = s // GRID_W
    rb = min(rb, rows)
    hg = NA_WIDTH // LANES
    kern = functools.partial(_na_kernel, rb=rb, rows=rows)
    n_keys = WIN_H * GRID_W + lc
    return pl.pallas_call(
        kern,
        grid=(b, hg, rows // rb),
        in_specs=[pl.BlockSpec((1, rb * GRID_W, LANES), lambda bi, g, i: (bi, i, g)),
                  pl.BlockSpec((1, s, LANES), lambda bi, g, i: (bi, 0, hg + g)),
                  pl.BlockSpec((1, s, LANES), lambda bi, g, i: (bi, 0, 2 * hg + g)),
                  pl.BlockSpec((1, lc, LANES), lambda bi, g, i: (bi, 0, hg + g)),
                  pl.BlockSpec((1, lc, LANES), lambda bi, g, i: (bi, 0, 2 * hg + g)),
                  pl.BlockSpec((WIN_H, 1, 2 * GRID_W, WIN_H * GRID_W), lambda bi, g, i: (0, g, 0, 0))],
        out_specs=pl.BlockSpec((1, rb * GRID_W, LANES), lambda bi, g, i: (bi, i, g)),
        out_shape=jax.ShapeDtypeStruct((b, s, NA_WIDTH), BF16),
        scratch_shapes=[pltpu.VMEM((2, 2 * GRID_W, n_keys), F32),
                        pltpu.VMEM((2, 2 * GRID_W, n_keys), BF16),
                        pltpu.VMEM((2, 2 * GRID_W, 1), F32)],
        compiler_params=_params(("parallel", "parallel", "arbitrary")),
        name="na",
    )(p, p, p, p_c, p_c, bias)


def _na_bias_table(rpb):
    qcol = jnp.arange(GRID_W)[:, None]
    kcol = jnp.arange(GRID_W)[None, :]
    c_start = jnp.clip(qcol - WIN_W // 2, 0, GRID_W - WIN_W)
    in_win = (kcol >= c_start) & (kcol < c_start + WIN_W)
    dc = jnp.clip(kcol - qcol + (WIN_W - 1), 0, 2 * WIN_W - 2)
    band = jnp.where(in_win[None, None], rpb[:, :, dc], MASK_VALUE)
    dr = jnp.arange(WIN_H)[:, None] + jnp.arange(WIN_H)[None, :]
    t = band[:, dr]
    t = t.transpose(1, 0, 3, 2, 4)
    return t.reshape(WIN_H, NA_HEADS // 2, 2 * GRID_W, WIN_H * GRID_W).astype(F32)


def _gla_kernel(*refs, reverse, has_init, emit_o, combine, emit_state):
    refs = list(refs)
    q_ref, k_ref, v_ref, lr_ref, w2_ref, b2_ref = refs[:6]
    pos = 6
    s0_ref = ob_ref = ng_ref = out_ref = sfin_ref = None
    if has_init:
        s0_ref = refs[pos]; pos += 1
    if combine:
        ob_ref, ng_ref = refs[pos], refs[pos + 1]; pos += 2
    if emit_o:
        out_ref = refs[pos]; pos += 1
    if emit_state:
        sfin_ref = refs[pos]; pos += 1
    s_ref, kend_buf, dec_buf, att_buf, qin_buf = refs[pos:pos + 5]

    j = pl.program_id(2)
    nblk = pl.num_programs(2)
    blk_len = GLA_BLOCK
    half = blk_len // 2
    c_len = GLA_CHUNK

    @pl.when(j == 0)
    def _():
        if has_init:
            s_ref[...] = s0_ref[0, 0]
        else:
            s_ref[...] = jnp.zeros_like(s_ref)

    ri = lax.broadcasted_iota(jnp.int32, (blk_len, blk_len), 0)
    ci = lax.broadcasted_iota(jnp.int32, (blk_len, blk_len), 1)
    tri = (ci >= ri) if reverse else (ci <= ri)
    tri_b = tri.astype(BF16)
    tri_h = tri[:half, :half]
    contract_last = (((1,), (1,)), ((), ()))
    contract_first = (((0,), (0,)), ((), ()))

    def bcast_halves(lo_row, hi_row):
        return jnp.concatenate([jnp.broadcast_to(lo_row, (half, lo_row.shape[1])),
                                jnp.broadcast_to(hi_row, (half, hi_row.shape[1]))], axis=0)

    def prepare(r0, slot):
        rows = slice(r0, r0 + blk_len)
        z = jnp.dot(lr_ref[0, rows, :].astype(BF16), w2_ref[0], preferred_element_type=F32) + b2_ref[0]
        g = (jnp.minimum(z, 0.0) - jnp.log1p(jnp.exp(-jnp.abs(z)))) * (1.0 / GATE_NORMALIZER)
        g_hi = g.astype(BF16)
        g_lo = (g - g_hi.astype(F32)).astype(BF16)
        cum = (jnp.dot(tri_b, g_hi, preferred_element_type=F32)
               + jnp.dot(tri_b, g_lo, preferred_element_type=F32))

        def row(i):
            return cum[i:i + 1, :]

        if reverse:
            m_lo, m_hi, mid, end = row(c_len), row(half + c_len), row(half), row(0)
        else:
            m_lo, m_hi, mid, end = row(c_len - 1), row(half + c_len - 1), row(half - 1), row(blk_len - 1)
        m_ref = bcast_halves(m_lo, m_hi)
        km = k_ref[0, rows, :].astype(F32) * jnp.exp(m_ref - cum)
        kend_buf[slot] = (km * bcast_halves(jnp.exp(end - m_lo), jnp.exp(end - m_hi))).astype(BF16)
        dec_buf[slot] = jnp.exp(end)
        if emit_o:
            qm = q_ref[0, rows, :].astype(F32) * jnp.exp(cum - m_ref)
            qm_b = qm.astype(BF16)
            km_b = km.astype(BF16)
            a_lo = lax.dot_general(qm_b[:half], km_b[:half], contract_last, preferred_element_type=F32)
            a_hi = lax.dot_general(qm_b[half:], km_b[half:], contract_last, preferred_element_type=F32)
            a_lo = jnp.where(tri_h, a_lo, 0.0)
            a_hi = jnp.where(tri_h, a_hi, 0.0)
            zeros = jnp.zeros((half, half), F32)
            if reverse:
                q_off = (qm[:half] * jnp.exp(m_lo - mid)).astype(BF16)
                k_off = (km[half:] * jnp.exp(mid - m_hi)).astype(BF16)
                a_off = lax.dot_general(q_off, k_off, contract_last, preferred_element_type=F32)
                att = jnp.concatenate([jnp.concatenate([a_lo, a_off], axis=1),
                                       jnp.concatenate([zeros, a_hi], axis=1)], axis=0)
            else:
                q_off = (qm[half:] * jnp.exp(m_hi - mid)).astype(BF16)
                k_off = (km[:half] * jnp.exp(mid - m_lo)).astype(BF16)
                a_off = lax.dot_general(q_off, k_off, contract_last, preferred_element_type=F32)
                att = jnp.concatenate([jnp.concatenate([a_lo, zeros], axis=1),
                                       jnp.concatenate([a_off, a_hi], axis=1)], axis=0)
            att_buf[slot] = att.astype(BF16)
            qin_buf[slot] = (qm * bcast_halves(jnp.exp(m_lo), jnp.exp(m_hi))).astype(BF16)

    def advance(r0, slot):
        rows = slice(r0, r0 + blk_len)
        v = v_ref[0, rows, :]
        st = s_ref[...]
        if emit_o:
            o = (jnp.dot(att_buf[slot], v, preferred_element_type=F32)
                 + lax.dot_general(qin_buf[slot], st.astype(BF16), contract_last, preferred_element_type=F32))
            if combine:
                tot = o + ob_ref[0, rows, :].astype(F32)
                ms = jnp.mean(tot * tot, axis=-1, keepdims=True)
                o = tot * lax.rsqrt(ms + RMS_EPS) * ng_ref[...]
            out_ref[0, rows, :] = o.astype(out_ref.dtype)
        s_ref[...] = st * dec_buf[slot] + lax.dot_general(v, kend_buf[slot], contract_first,
                                                          preferred_element_type=F32)

    nb = q_ref.shape[1] // blk_len
    order = list(range(nb - 1, -1, -1) if reverse else range(nb))
    prepare(order[0] * blk_len, 0)
    for n, bi in enumerate(order):
        if n + 1 < nb:
            prepare(order[n + 1] * blk_len, (n + 1) % 2)
        advance(bi * blk_len, n % 2)

    if emit_state:
        @pl.when(j == nblk - 1)
        def _():
            sfin_ref[0, 0] = s_ref[...]


def _gla_scan(p, g_lr, w2e, b2e, *, direction, tb, q_blk, k_blk, v_blk, dk, dv,
              s0=None, ob=None, norm_g=None, emit_o=True, emit_state=False):
    b, t, _ = p.shape
    nblk = t // tb
    reverse = direction == 1
    combine = ob is not None
    has_init = s0 is not None

    def blk(j):
        return (nblk - 1 - j) if reverse else j

    in_specs = [pl.BlockSpec((1, tb, dk), lambda bi, h, j: (bi, blk(j), q_blk + h)),
                pl.BlockSpec((1, tb, dk), lambda bi, h, j: (bi, blk(j), k_blk + h)),
                pl.BlockSpec((1, tb, dv), lambda bi, h, j: (bi, blk(j), v_blk + h)),
                pl.BlockSpec((1, tb, LANES), lambda bi, h, j: (bi, blk(j), 0)),
                pl.BlockSpec((1, LANES, dk), lambda bi, h, j: (direction, 0, h)),
                pl.BlockSpec((1, 1, dk), lambda bi, h, j: (direction, 0, h))]
    args = [p, p, p, g_lr, w2e, b2e]
    if has_init:
        in_specs.append(pl.BlockSpec((1, 1, dv, dk), lambda bi, h, j: (bi, h, 0, 0)))
        args.append(s0)
    if combine:
        in_specs.append(pl.BlockSpec((1, tb, dv), lambda bi, h, j: (bi, blk(j), h)))
        in_specs.append(pl.BlockSpec((1, dv), lambda bi, h, j: (0, 0)))
        args += [ob, norm_g]
    out_specs, out_shape = [], []
    if emit_o:
        out_specs.append(pl.BlockSpec((1, tb, dv), lambda bi, h, j: (bi, blk(j), h)))
        out_shape.append(jax.ShapeDtypeStruct((b, t, GLA_HEADS * dv), BF16))
    if emit_state:
        out_specs.append(pl.BlockSpec((1, 1, dv, dk), lambda bi, h, j: (bi, h, 0, 0)))
        out_shape.append(jax.ShapeDtypeStruct((b, GLA_HEADS, dv, dk), F32))
    assert tb % GLA_BLOCK == 0
    kern = functools.partial(_gla_kernel, reverse=reverse, has_init=has_init,
                             emit_o=emit_o, combine=combine, emit_state=emit_state)
    res = pl.pallas_call(
        kern,
        grid=(b, GLA_HEADS, nblk),
        in_specs=in_specs,
        out_specs=out_specs,
        out_shape=out_shape,
        scratch_shapes=[pltpu.VMEM((dv, dk), F32),
                        pltpu.VMEM((2, GLA_BLOCK, dk), BF16),
                        pltpu.VMEM((2, 1, dk), F32),
                        pltpu.VMEM((2, GLA_BLOCK, GLA_BLOCK), BF16),
                        pltpu.VMEM((2, GLA_BLOCK, dk), BF16)],
        compiler_params=_params(("parallel", "parallel", "arbitrary")),
        name="gla_" + ("bwd" if reverse else "fwd") + ("_ctx" if not emit_o else ""),
    )(*args)
    return res[0] if len(res) == 1 else res


def _merge_kernel(ya_ref, az_ref, yb_ref, bz_ref, ga_ref, gb_ref, x_ref, mod_ref,
                  wa_ref, wb_ref, wo_ref, lng_ref, lnb_ref, o_ref):
    ua = (ya_ref[0].astype(F32) * az_ref[0].astype(F32)).astype(BF16)
    ub = (yb_ref[0].astype(F32) * bz_ref[0].astype(F32)).astype(BF16)
    pa = jnp.dot(ua, wa_ref[...], preferred_element_type=F32)
    pb = jnp.dot(ub, wb_ref[...], preferred_element_type=F32)
    m = (ga_ref[0].astype(F32) * pa + gb_ref[0].astype(F32) * pb).astype(BF16)
    out = jnp.dot(m, wo_ref[...], preferred_element_type=F32)
    z = DEEPNORM_ALPHA * x_ref[0] + mod_ref[0, 2:3, :] * out
    mu = jnp.mean(z, axis=-1, keepdims=True)
    zc = z - mu
    var = jnp.mean(zc * zc, axis=-1, keepdims=True)
    o_ref[0] = zc * lax.rsqrt(var + LN_EPS) * lng_ref[...] + lnb_ref[...]


def _merge(y_a, y_b, p, x, mod, wa, wb, wo, ln_g, ln_b, az_blk, bz_blk, ga_blk, gb_blk, tm=256):
    b, s, d = x.shape
    na = y_a.shape[2]
    dv = y_b.shape[2]
    const = dict(pipeline_mode=pl.Buffered(1))
    return pl.pallas_call(
        _merge_kernel,
        grid=(b, s // tm),
        in_specs=[pl.BlockSpec((1, tm, na), lambda bi, i: (bi, i, 0)),
                  pl.BlockSpec((1, tm, na), lambda bi, i: (bi, i, az_blk)),
                  pl.BlockSpec((1, tm, dv), lambda bi, i: (bi, i, 0)),
                  pl.BlockSpec((1, tm, dv), lambda bi, i: (bi, i, bz_blk)),
                  pl.BlockSpec((1, tm, d), lambda bi, i: (bi, i, ga_blk)),
                  pl.BlockSpec((1, tm, d), lambda bi, i: (bi, i, gb_blk)),
                  pl.BlockSpec((1, tm, d), lambda bi, i: (bi, i, 0)),
                  pl.BlockSpec((1, 3, d), lambda bi, i: (bi, 0, 0)),
                  pl.BlockSpec((na, d), lambda bi, i: (0, 0), **const),
                  pl.BlockSpec((dv, d), lambda bi, i: (0, 0), **const),
                  pl.BlockSpec((d, d), lambda bi, i: (0, 0), **const),
                  pl.BlockSpec((1, d), lambda bi, i: (0, 0)),
                  pl.BlockSpec((1, d), lambda bi, i: (0, 0))],
        out_specs=pl.BlockSpec((1, tm, d), lambda bi, i: (bi, i, 0)),
        out_shape=jax.ShapeDtypeStruct((b, s, d), F32),
        compiler_params=_params(("parallel", "parallel")),
        name="merge",
    )(y_a, p, y_b, p, p, p, x, mod, wa, wb, wo, ln_g, ln_b)


def _rope_tables(s):
    pairs = LANES // 2
    pos = jnp.arange(s)
    row = (pos // GRID_W).astype(F32)
    col = (pos % GRID_W).astype(F32)
    inv_freq = ROPE_BASE ** (-jnp.arange(pairs, dtype=F32) / pairs)
    ar = row[:, None] * inv_freq[None, :]
    ac = col[:, None] * inv_freq[None, :]
    cos_tab = jnp.concatenate([jnp.cos(ar), jnp.cos(ar), jnp.cos(ac), jnp.cos(ac)], axis=-1)
    sin_tab = jnp.concatenate([-jnp.sin(ar), jnp.sin(ar), -jnp.sin(ac), jnp.sin(ac)], axis=-1)
    return cos_tab, sin_tab


def _expand_gate(w_gate2, b_gate, dk):
    dirs, rank, _ = w_gate2.shape
    pairs = dk // 4

    def dup(a):
        a = a.reshape(a.shape[:-1] + (GLA_HEADS, 2, pairs))
        a = jnp.concatenate([a, a], axis=-1)
        return a.reshape(a.shape[:-3] + (GLA_HEADS * dk,))

    w = dup(w_gate2.astype(F32))
    w2e = jnp.zeros((dirs, LANES, GLA_HEADS * dk), F32)
    for d in range(dirs):
        w2e = w2e.at[d, d * rank:(d + 1) * rank].set(w[d])
    b2e = dup(b_gate.astype(F32))[:, None, :]
    return w2e, b2e


def kernel(x, c, ctx, c_ctx, w_mod, b_mod, w_in, na_rpb, gla_w_gate2, gla_b_gate, gla_norm_g,
           w_br_a, w_br_b, w_out, ln_g, ln_b):
    bsz, s, d = x.shape
    lc = ctx.shape[1]
    key_dim, val_dim = d // 2, d
    dk, dv = key_dim // GLA_HEADS, val_dim // GLA_HEADS
    lyr = 0

    widths = (NA_WIDTH,) * 4 + (key_dim, key_dim, val_dim, val_dim, N_DIRS * GATE_RANK, d, d)
    offs = [0]
    for w_ in widths:
        offs.append(offs[-1] + w_)
    w_full = w_in[lyr]
    cols = [w_full[:, offs[i]:offs[i + 1]] for i in range(len(widths))]
    aq, ak, av, az, bq, bk, bv, bz, bg, mga, mgb = cols
    w_main = jnp.concatenate([aq, ak, av, az, bq, bk, bv, bz, mga, mgb], axis=1).astype(BF16)
    w_gate1 = jnp.pad(bg, ((0, 0), (0, LANES - bg.shape[1]))).astype(BF16)
    tn = 1024
    kinds = (["na_q"] * (NA_WIDTH // tn) + ["plain"] * (2 * NA_WIDTH // tn) + ["silu"] * (NA_WIDTH // tn)
             + ["gla_q"] * (key_dim // tn) + ["gla_k"] * (key_dim // tn) + ["plain"] * (val_dim // tn)
             + ["silu"] * (val_dim // tn) + ["sigmoid"] * (2 * d // tn))
    o_az = 3 * NA_WIDTH
    o_bq = 4 * NA_WIDTH
    o_bk = o_bq + key_dim
    o_bv = o_bk + key_dim
    o_bz = o_bv + val_dim
    o_ga = o_bz + val_dim
    o_gb = o_ga + d

    cs = jnp.concatenate([c, c_ctx[None], jnp.zeros((MOD_ROWS - bsz - 1, d), F32)], axis=0)
    mod = _modulation(cs, w_mod[lyr], b_mod[lyr][None])
    mod_lat = mod[:bsz].reshape(bsz, 3, d)
    mod_ctx = jnp.broadcast_to(mod[bsz].reshape(1, 3, d), (bsz, 3, d))

    cos_tab, sin_tab = _rope_tables(s)
    p, g_lr = _projection(x, mod_lat, w_main, w_gate1, cos_tab, sin_tab, kinds, tm=min(1024, s), tn=tn)
    ones_tab = jnp.ones((lc, 2 * LANES), F32)
    p_c, g_lr_c = _projection(ctx, mod_ctx, w_main, w_gate1, ones_tab, jnp.zeros_like(ones_tab), kinds,
                              tm=lc, tn=tn)

    y_a = _neighbourhood_attention(p, p_c, _na_bias_table(na_rpb[lyr]))

    w2e, b2e = _expand_gate(gla_w_gate2[lyr], gla_b_gate[lyr], dk)
    w2e = w2e.astype(BF16)
    blocks = dict(q_blk=o_bq // dk, k_blk=o_bk // dk, v_blk=o_bv // dv, dk=dk, dv=dv)
    s_f = _gla_scan(p_c, g_lr_c, w2e, b2e, direction=0, tb=lc, emit_o=False, emit_state=True, **blocks)
    s_b = _gla_scan(p_c, g_lr_c, w2e, b2e, direction=1, tb=lc, emit_o=False, emit_state=True, **blocks)
    tb = min(1024, s)
    o_b = _gla_scan(p, g_lr, w2e, b2e, direction=1, tb=tb, s0=s_b, **blocks)
    y_b = _gla_scan(p, g_lr, w2e, b2e, direction=0, tb=tb, s0=s_f, ob=o_b,
                    norm_g=gla_norm_g[lyr][None].astype(F32), **blocks)

    return _merge(y_a, y_b, p, x, mod_lat,
                  w_br_a[lyr].astype(BF16), w_br_b[lyr].astype(BF16), w_out[lyr].astype(BF16),
                  ln_g[lyr][None], ln_b[lyr][None],
                  az_blk=o_az // NA_WIDTH, bz_blk=o_bz // val_dim, ga_blk=o_ga // d, gb_blk=o_gb // d)
```

```python
import functools

import jax
import jax.numpy as jnp
from jax import lax
from jax.experimental import pallas as pl
from jax.experimental.pallas import tpu as pltpu

F32 = jnp.float32
BF16 = jnp.bfloat16

GRID_W = 64
WIN_H = 8
WIN_W = 16
NA_HEADS = 16
NA_HEAD_DIM = 64
NA_WIDTH = NA_HEADS * NA_HEAD_DIM
GLA_HEADS = 4
GATE_RANK = 16
GATE_NORMALIZER = 16.0
GLA_CHUNK = 64
GLA_BLOCK = 4 * GLA_CHUNK
N_DIRS = 2
ROPE_BASE = 10000.0
DEPTH = 1
DEEPNORM_ALPHA = (2 * DEPTH) ** 0.25
LN_EPS = 1e-6
RMS_EPS = 1e-6
MASK_VALUE = -1e30

LANES = 128
MOD_ROWS = 8
VMEM_LIMIT = 56 * 2 ** 20


def _params(sem, vmem=VMEM_LIMIT):
    return pltpu.CompilerParams(dimension_semantics=sem, vmem_limit_bytes=vmem)


def _mod_kernel(c_ref, w_ref, b_ref, o_ref):
    c = c_ref[...]
    s = c * jax.nn.sigmoid(c)
    o_ref[...] = jnp.dot(s, w_ref[...], preferred_element_type=F32) + b_ref[...]


def _modulation(cs, w, b, tn=768):
    rows, d = cs.shape
    n = w.shape[1]
    return pl.pallas_call(
        _mod_kernel,
        grid=(n // tn,),
        in_specs=[pl.BlockSpec((rows, d), lambda j: (0, 0)),
                  pl.BlockSpec((d, tn), lambda j: (0, j)),
                  pl.BlockSpec((1, tn), lambda j: (0, j))],
        out_specs=pl.BlockSpec((rows, tn), lambda j: (0, j)),
        out_shape=jax.ShapeDtypeStruct((rows, n), F32),
        compiler_params=_params(("parallel",)),
        name="mod",
    )(cs, w, b)


def _ln_proj_kernel(x_ref, mod_ref, w_ref, wg_ref, p_ref, g_ref, h_ref, *, ln_rows, scaled_tiles, scale):
    n = pl.program_id(2)
    tm = x_ref.shape[1]

    @pl.when(n == 0)
    def _():
        shift = mod_ref[0, 0:1, :]
        scale1 = 1.0 + mod_ref[0, 1:2, :]
        for r0 in range(0, tm, ln_rows):
            xs = x_ref[0, r0:r0 + ln_rows, :]
            mu = jnp.mean(xs, axis=-1, keepdims=True)
            xc = xs - mu
            var = jnp.mean(xc * xc, axis=-1, keepdims=True)
            hb = (xc * lax.rsqrt(var + LN_EPS) * scale1 + shift).astype(BF16)
            h_ref[0, r0:r0 + ln_rows, :] = hb
            g_ref[0, r0:r0 + ln_rows, :] = jnp.dot(hb, wg_ref[...], preferred_element_type=F32)

    acc = jnp.dot(h_ref[0], w_ref[...], preferred_element_type=F32)
    p_ref[0] = (acc * jnp.where(n < scaled_tiles, scale, 1.0)).astype(p_ref.dtype)


def _ln_projection(xin, mod, w, wg, tm, tn, scaled_tiles, scale, ln_rows=256):
    b, t, d = xin.shape
    n = w.shape[1]
    kern = functools.partial(_ln_proj_kernel, ln_rows=min(ln_rows, tm), scaled_tiles=scaled_tiles, scale=scale)
    return pl.pallas_call(
        kern,
        grid=(b, t // tm, n // tn),
        in_specs=[pl.BlockSpec((1, tm, d), lambda bi, i, j: (bi, i, 0)),
                  pl.BlockSpec((1, 3, d), lambda bi, i, j: (bi, 0, 0)),
                  pl.BlockSpec((d, tn), lambda bi, i, j: (0, j)),
                  pl.BlockSpec((d, LANES), lambda bi, i, j: (0, 0))],
        out_specs=[pl.BlockSpec((1, tm, tn), lambda bi, i, j: (bi, i, j)),
                   pl.BlockSpec((1, tm, LANES), lambda bi, i, j: (bi, i, 0)),
                   pl.BlockSpec((1, tm, d), lambda bi, i, j: (bi, i, 0))],
        out_shape=[jax.ShapeDtypeStruct((b, t, n), BF16),
                   jax.ShapeDtypeStruct((b, t, LANES), F32),
                   jax.ShapeDtypeStruct((b, t, d), BF16)],
        compiler_params=_params(("parallel", "parallel", "arbitrary")),
        name="ln_proj",
    )(xin, mod, w, wg)


def _act_proj_kernel(*refs, kind, scaled_tiles, scale):
    if kind == "rope":
        h_ref, w_ref, rowtab_ref, coltab_ref, o_ref = refs
    else:
        h_ref, w_ref, o_ref = refs
    acc = jnp.dot(h_ref[0], w_ref[...], preferred_element_type=F32)
    if kind == "silu":
        o_ref[0] = (acc * jax.nn.sigmoid(acc)).astype(o_ref.dtype)
    elif kind == "sigmoid":
        o_ref[0] = jax.nn.sigmoid(acc).astype(o_ref.dtype)
    else:
        acc = acc * jnp.where(pl.program_id(2) < scaled_tiles, scale, 1.0)
        for r in range(acc.shape[0] // GRID_W):
            rows = slice(r * GRID_W, (r + 1) * GRID_W)
            for j in range(acc.shape[1] // LANES):
                lanes = slice(j * LANES, (j + 1) * LANES)
                u = acc[rows, lanes]
                if j % 2 == 0:
                    c, s = rowtab_ref[r:r + 1, :LANES], rowtab_ref[r:r + 1, LANES:]
                else:
                    c, s = coltab_ref[:, :LANES], coltab_ref[:, LANES:]
                o_ref[0, rows, lanes] = (u * c + pltpu.roll(u, LANES // 2, 1) * s).astype(o_ref.dtype)


def _act_projection(h, w, kind, tm, tn, tables=(), scaled_tiles=0, scale=1.0):
    b, t, d = h.shape
    n = w.shape[1]
    kern = functools.partial(_act_proj_kernel, kind=kind, scaled_tiles=scaled_tiles, scale=scale)
    in_specs = [pl.BlockSpec((1, tm, d), lambda bi, i, j: (bi, i, 0)),
                pl.BlockSpec((d, tn), lambda bi, i, j: (0, j))]
    if tables:
        in_specs += [pl.BlockSpec((tm // GRID_W, 2 * LANES), lambda bi, i, j: (i, 0)),
                     pl.BlockSpec((GRID_W, 2 * LANES), lambda bi, i, j: (0, 0))]
    return pl.pallas_call(
        kern,
        grid=(b, t // tm, n // tn),
        in_specs=in_specs,
        out_specs=pl.BlockSpec((1, tm, tn), lambda bi, i, j: (bi, i, j)),
        out_shape=jax.ShapeDtypeStruct((b, t, n), BF16),
        compiler_params=_params(("parallel", "parallel", "arbitrary")),
        name="proj_" + kind,
    )(h, w, *tables)


def _na_kernel(q_ref, k_ref, v_ref, kc_ref, vc_ref, bias_ref, o_ref, s_buf, p_buf, l_buf, *, rb, rows):
    blk = pl.program_id(2)
    hd = NA_HEAD_DIM
    n_loc = WIN_H * GRID_W
    is_a = lax.broadcasted_iota(jnp.int32, (GRID_W, 2 * hd), 1) < hd
    contract_last = (((1,), (1,)), ((), ()))

    def window_start(i):
        r = blk * rb + i
        r_start = jnp.clip(r - WIN_H // 2, 0, rows - WIN_H)
        return r_start - r + (WIN_H - 1), pl.multiple_of(r_start * GRID_W, GRID_W)

    def scores(i):
        cls, k0 = window_start(i)
        q = q_ref[0, i * GRID_W:(i + 1) * GRID_W, :]
        zero = jnp.zeros_like(q)
        qs = jnp.concatenate([jnp.where(is_a, q, zero), jnp.where(is_a, zero, q)], axis=0)
        k8 = k_ref[0, pl.ds(k0, n_loc), :]
        bias = jnp.concatenate([bias_ref[0, cls + 2 * jj] for jj in range(WIN_H // 2)], axis=1)
        s_buf[i % 2, :, :n_loc] = lax.dot_general(qs, k8, contract_last, preferred_element_type=F32) + bias
        s_buf[i % 2, :, n_loc:] = lax.dot_general(qs, kc_ref[0], contract_last, preferred_element_type=F32)

    def softmax(i):
        s = s_buf[i % 2]
        m = jnp.max(s, axis=-1, keepdims=True)
        p = jnp.exp(s - m)
        l_buf[i % 2] = jnp.sum(p, axis=-1, keepdims=True)
        p_buf[i % 2] = p.astype(BF16)

    def values(i):
        _, k0 = window_start(i)
        v8 = v_ref[0, pl.ds(k0, n_loc), :]
        o = (jnp.dot(p_buf[i % 2, :, :n_loc], v8, preferred_element_type=F32)
             + jnp.dot(p_buf[i % 2, :, n_loc:], vc_ref[0], preferred_element_type=F32))
        o = o / l_buf[i % 2]
        out = jnp.where(is_a, o[:GRID_W], o[GRID_W:])
        o_ref[0, i * GRID_W:(i + 1) * GRID_W, :] = out.astype(o_ref.dtype)

    for t in range(rb + 2):
        if t >= 2:
            values(t - 2)
        if 1 <= t <= rb:
            softmax(t - 1)
        if t < rb:
            scores(t)


def _neighbourhood_attention(p, p_c, bias, rb=16):
    b, s, _ = p.shape
    lc = p_c.shape[1]
    rows = s // GRID_W
    rb = min(rb, rows)
    hg = NA_WIDTH // LANES
    kern = functools.partial(_na_kernel, rb=rb, rows=rows)
    n_keys = WIN_H * GRID_W + lc
    return pl.pallas_call(
        kern,
        grid=(b, hg, rows // rb),
        in_specs=[pl.BlockSpec((1, rb * GRID_W, LANES), lambda bi, g, i: (bi, i, g)),
                  pl.BlockSpec((1, s, LANES), lambda bi, g, i: (bi, 0, hg + g)),
                  pl.BlockSpec((1, s, LANES), lambda bi, g, i: (bi, 0, 2 * hg + g)),
                  pl.BlockSpec((1, lc, LANES), lambda bi, g, i: (bi, 0, hg + g)),
                  pl.BlockSpec((1, lc, LANES), lambda bi, g, i: (bi, 0, 2 * hg + g)),
                  pl.BlockSpec((1, 2 * WIN_H - 2, 2 * GRID_W, 2 * GRID_W), lambda bi, g, i: (g, 0, 0, 0))],
        out_specs=pl.BlockSpec((1, rb * GRID_W, LANES), lambda bi, g, i: (bi, i, g)),
        out_shape=jax.ShapeDtypeStruct((b, s, NA_WIDTH), BF16),
        scratch_shapes=[pltpu.VMEM((2, 2 * GRID_W, n_keys), F32),
                        pltpu.VMEM((2, 2 * GRID_W, n_keys), BF16),
                        pltpu.VMEM((2, 2 * GRID_W, 1), F32)],
        compiler_params=_params(("parallel", "parallel", "arbitrary")),
        name="na",
    )(p, p, p, p_c, p_c, bias)


def _na_bias_table(rpb):
    qcol = jnp.arange(GRID_W)[:, None]
    kcol = jnp.arange(GRID_W)[None, :]
    c_start = jnp.clip(qcol - WIN_W // 2, 0, GRID_W - WIN_W)
    in_win = (kcol >= c_start) & (kcol < c_start + WIN_W)
    dc = jnp.clip(kcol - qcol + (WIN_W - 1), 0, 2 * WIN_W - 2)
    band = jnp.where(in_win[None, None], rpb[:, :, dc], MASK_VALUE).astype(F32)
    n_dr = band.shape[1]
    band = band.reshape(NA_HEADS // 2, 2, n_dr, GRID_W, GRID_W).transpose(0, 2, 1, 3, 4)
    band = band.reshape(NA_HEADS // 2, n_dr, 2 * GRID_W, GRID_W)
    return jnp.concatenate([band[:, :-1], band[:, 1:]], axis=-1)


def _gla_kernel(*refs, reverse, has_init, emit_o, combine, emit_state):
    refs = list(refs)
    q_ref, k_ref, v_ref, lr_ref, w2_ref, b2_ref = refs[:6]
    pos = 6
    s0_ref = ob_ref = ng_ref = out_ref = sfin_ref = None
    if has_init:
        s0_ref = refs[pos]; pos += 1
    if combine:
        ob_ref, ng_ref = refs[pos], refs[pos + 1]; pos += 2
    if emit_o:
        out_ref = refs[pos]; pos += 1
    if emit_state:
        sfin_ref = refs[pos]; pos += 1
    s_ref, cum_buf, kend_buf, dec_buf, att_buf, qin_buf = refs[pos:pos + 6]

    j = pl.program_id(2)
    nblk = pl.num_programs(2)
    blk_len = GLA_BLOCK
    half = blk_len // 2
    c_len = GLA_CHUNK

    @pl.when(j == 0)
    def _():
        if has_init:
            s_ref[...] = s0_ref[0, 0]
        else:
            s_ref[...] = jnp.zeros_like(s_ref)

    ri = lax.broadcasted_iota(jnp.int32, (blk_len, blk_len), 0)
    ci = lax.broadcasted_iota(jnp.int32, (blk_len, blk_len), 1)
    tri = (ci >= ri) if reverse else (ci <= ri)
    tri_b = tri.astype(BF16)
    tri_h = tri[:half, :half]
    contract_last = (((1,), (1,)), ((), ()))
    contract_first = (((0,), (0,)), ((), ()))

    def bcast_halves(lo_row, hi_row):
        return jnp.concatenate([jnp.broadcast_to(lo_row, (half, lo_row.shape[1])),
                                jnp.broadcast_to(hi_row, (half, hi_row.shape[1]))], axis=0)

    def decay(r0):
        rows = slice(r0, r0 + blk_len)
        z = jnp.dot(lr_ref[0, rows, :].astype(BF16), w2_ref[0], preferred_element_type=F32) + b2_ref[0]
        g = (jnp.minimum(z, 0.0) - jnp.log(1.0 + jnp.exp(-jnp.abs(z)))) * (1.0 / GATE_NORMALIZER)
        g_hi = g.astype(BF16)
        g_lo = (g - g_hi.astype(F32)).astype(BF16)
        cum_buf[rows, :] = (jnp.dot(tri_b, g_hi, preferred_element_type=F32)
                            + jnp.dot(tri_b, g_lo, preferred_element_type=F32))

    def prepare(r0, slot):
        rows = slice(r0, r0 + blk_len)
        cum = cum_buf[rows, :]

        def row(i):
            return cum_buf[r0 + i:r0 + i + 1, :]

        if reverse:
            m_lo, m_hi, mid, end = row(c_len), row(half + c_len), row(half), row(0)
        else:
            m_lo, m_hi, mid, end = row(c_len - 1), row(half + c_len - 1), row(half - 1), row(blk_len - 1)
        m_ref = bcast_halves(m_lo, m_hi)
        def expb(a):
            return jnp.exp(a).astype(BF16)

        km = k_ref[0, rows, :] * expb(m_ref - cum)
        kend_buf[slot] = km * bcast_halves(expb(end - m_lo), expb(end - m_hi))
        dec_buf[slot] = jnp.exp(end)
        if emit_o:
            qm = q_ref[0, rows, :] * expb(cum - m_ref)
            a_lo = lax.dot_general(qm[:half], km[:half], contract_last, preferred_element_type=F32)
            a_hi = lax.dot_general(qm[half:], km[half:], contract_last, preferred_element_type=F32)
            a_lo = jnp.where(tri_h, a_lo, 0.0)
            a_hi = jnp.where(tri_h, a_hi, 0.0)
            zeros = jnp.zeros((half, half), F32)
            if reverse:
                a_off = lax.dot_general(qm[:half] * expb(m_lo - mid), km[half:] * expb(mid - m_hi),
                                        contract_last, preferred_element_type=F32)
                att = jnp.concatenate([jnp.concatenate([a_lo, a_off], axis=1),
                                       jnp.concatenate([zeros, a_hi], axis=1)], axis=0)
            else:
                a_off = lax.dot_general(qm[half:] * expb(m_hi - mid), km[:half] * expb(mid - m_lo),
                                        contract_last, preferred_element_type=F32)
                att = jnp.concatenate([jnp.concatenate([a_lo, zeros], axis=1),
                                       jnp.concatenate([a_off, a_hi], axis=1)], axis=0)
            att_buf[slot] = att.astype(BF16)
            qin_buf[slot] = qm * bcast_halves(expb(m_lo), expb(m_hi))

    def advance(r0, slot):
        rows = slice(r0, r0 + blk_len)
        v = v_ref[0, rows, :]
        st = s_ref[...]
        if emit_o:
            o = (jnp.dot(att_buf[slot], v, preferred_element_type=F32)
                 + lax.dot_general(qin_buf[slot], st.astype(BF16), contract_last, preferred_element_type=F32))
            if combine:
                tot = o + ob_ref[0, rows, :].astype(F32)
                ms = jnp.mean(tot * tot, axis=-1, keepdims=True)
                o = tot * lax.rsqrt(ms + RMS_EPS) * ng_ref[...]
            out_ref[0, rows, :] = o.astype(out_ref.dtype)
        s_ref[...] = st * dec_buf[slot] + lax.dot_general(v, kend_buf[slot], contract_first,
                                                          preferred_element_type=F32)

    nb = q_ref.shape[1] // blk_len
    order = list(range(nb - 1, -1, -1) if reverse else range(nb))
    for bi in order:
        decay(bi * blk_len)
    prepare(order[0] * blk_len, 0)
    for n, bi in enumerate(order):
        if n + 1 < nb:
            prepare(order[n + 1] * blk_len, (n + 1) % 2)
        advance(bi * blk_len, n % 2)

    if emit_state:
        @pl.when(j == nblk - 1)
        def _():
            sfin_ref[0, 0] = s_ref[...]


def _gla_scan(p_qk, p_v, g_lr, w2e, b2e, *, direction, tb, q_blk, k_blk, v_blk, dk, dv,
              s0=None, ob=None, norm_g=None, emit_o=True, emit_state=False):
    b, t, _ = p_qk.shape
    nblk = t // tb
    reverse = direction == 1
    combine = ob is not None
    has_init = s0 is not None

    def blk(j):
        return (nblk - 1 - j) if reverse else j

    in_specs = [pl.BlockSpec((1, tb, dk), lambda bi, h, j: (bi, blk(j), q_blk + h)),
                pl.BlockSpec((1, tb, dk), lambda bi, h, j: (bi, blk(j), k_blk + h)),
                pl.BlockSpec((1, tb, dv), lambda bi, h, j: (bi, blk(j), v_blk + h)),
                pl.BlockSpec((1, tb, LANES), lambda bi, h, j: (bi, blk(j), 0)),
                pl.BlockSpec((1, LANES, dk), lambda bi, h, j: (direction, 0, h)),
                pl.BlockSpec((1, 1, dk), lambda bi, h, j: (direction, 0, h))]
    args = [p_qk, p_qk, p_v, g_lr, w2e, b2e]
    if has_init:
        in_specs.append(pl.BlockSpec((1, 1, dv, dk), lambda bi, h, j: (bi, h, 0, 0)))
        args.append(s0)
    if combine:
        in_specs.append(pl.BlockSpec((1, tb, dv), lambda bi, h, j: (bi, blk(j), h)))
        in_specs.append(pl.BlockSpec((1, dv), lambda bi, h, j: (0, 0)))
        args += [ob, norm_g]
    out_specs, out_shape = [], []
    if emit_o:
        out_specs.append(pl.BlockSpec((1, tb, dv), lambda bi, h, j: (bi, blk(j), h)))
        out_shape.append(jax.ShapeDtypeStruct((b, t, GLA_HEADS * dv), BF16))
    if emit_state:
        out_specs.append(pl.BlockSpec((1, 1, dv, dk), lambda bi, h, j: (bi, h, 0, 0)))
        out_shape.append(jax.ShapeDtypeStruct((b, GLA_HEADS, dv, dk), F32))
    assert tb % GLA_BLOCK == 0
    kern = functools.partial(_gla_kernel, reverse=reverse, has_init=has_init,
                             emit_o=emit_o, combine=combine, emit_state=emit_state)
    res = pl.pallas_call(
        kern,
        grid=(b, GLA_HEADS, nblk),
        in_specs=in_specs,
        out_specs=out_specs,
        out_shape=out_shape,
        scratch_shapes=[pltpu.VMEM((dv, dk), F32),
                        pltpu.VMEM((tb, dk), F32),
                        pltpu.VMEM((2, GLA_BLOCK, dk), BF16),
                        pltpu.VMEM((2, 1, dk), F32),
                        pltpu.VMEM((2, GLA_BLOCK, GLA_BLOCK), BF16),
                        pltpu.VMEM((2, GLA_BLOCK, dk), BF16)],
        compiler_params=_params(("parallel", "parallel", "arbitrary")),
        name="gla_" + ("bwd" if reverse else "fwd") + ("_ctx" if not emit_o else ""),
    )(*args)
    return res[0] if len(res) == 1 else res


def _merge_kernel(ya_ref, az_ref, yb_ref, bz_ref, ga_ref, gb_ref, x_ref, mod_ref,
                  wa_ref, wb_ref, wo_ref, lng_ref, lnb_ref, o_ref):
    ua = (ya_ref[0].astype(F32) * az_ref[0].astype(F32)).astype(BF16)
    ub = (yb_ref[0].astype(F32) * bz_ref[0].astype(F32)).astype(BF16)
    pa = jnp.dot(ua, wa_ref[...], preferred_element_type=F32)
    pb = jnp.dot(ub, wb_ref[...], preferred_element_type=F32)
    m = (ga_ref[0].astype(F32) * pa + gb_ref[0].astype(F32) * pb).astype(BF16)
    out = jnp.dot(m, wo_ref[...], preferred_element_type=F32)
    z = DEEPNORM_ALPHA * x_ref[0] + mod_ref[0, 2:3, :] * out
    mu = jnp.mean(z, axis=-1, keepdims=True)
    zc = z - mu
    var = jnp.mean(zc * zc, axis=-1, keepdims=True)
    o_ref[0] = zc * lax.rsqrt(var + LN_EPS) * lng_ref[...] + lnb_ref[...]


def _merge(y_a, y_b, p_silu, p_gate, x, mod, wa, wb, wo, ln_g, ln_b, az_blk, bz_blk, ga_blk, gb_blk, tm=256):
    b, s, d = x.shape
    na = y_a.shape[2]
    dv = y_b.shape[2]
    const = dict(pipeline_mode=pl.Buffered(1))
    return pl.pallas_call(
        _merge_kernel,
        grid=(b, s // tm),
        in_specs=[pl.BlockSpec((1, tm, na), lambda bi, i: (bi, i, 0)),
                  pl.BlockSpec((1, tm, na), lambda bi, i: (bi, i, az_blk)),
                  pl.BlockSpec((1, tm, dv), lambda bi, i: (bi, i, 0)),
                  pl.BlockSpec((1, tm, dv), lambda bi, i: (bi, i, bz_blk)),
                  pl.BlockSpec((1, tm, d), lambda bi, i: (bi, i, ga_blk)),
                  pl.BlockSpec((1, tm, d), lambda bi, i: (bi, i, gb_blk)),
                  pl.BlockSpec((1, tm, d), lambda bi, i: (bi, i, 0)),
                  pl.BlockSpec((1, 3, d), lambda bi, i: (bi, 0, 0)),
                  pl.BlockSpec((na, d), lambda bi, i: (0, 0), **const),
                  pl.BlockSpec((dv, d), lambda bi, i: (0, 0), **const),
                  pl.BlockSpec((d, d), lambda bi, i: (0, 0), **const),
                  pl.BlockSpec((1, d), lambda bi, i: (0, 0)),
                  pl.BlockSpec((1, d), lambda bi, i: (0, 0))],
        out_specs=pl.BlockSpec((1, tm, d), lambda bi, i: (bi, i, 0)),
        out_shape=jax.ShapeDtypeStruct((b, s, d), F32),
        compiler_params=_params(("parallel", "parallel")),
        name="merge",
    )(y_a, p_silu, y_b, p_silu, p_gate, p_gate, x, mod, wa, wb, wo, ln_g, ln_b)


def _rope_tables(s):
    pairs = LANES // 2
    inv_freq = ROPE_BASE ** (-jnp.arange(pairs, dtype=F32) / pairs)

    def table(n):
        ang = jnp.arange(n).astype(F32)[:, None] * inv_freq[None, :]
        return jnp.concatenate([jnp.cos(ang), jnp.cos(ang), -jnp.sin(ang), jnp.sin(ang)], axis=-1)

    return table(s // GRID_W), table(GRID_W)


def _identity_rope_tables(s):
    def table(n):
        return jnp.concatenate([jnp.ones((n, LANES), F32), jnp.zeros((n, LANES), F32)], axis=-1)

    return table(s // GRID_W), table(GRID_W)


def _expand_gate(w_gate2, b_gate, dk):
    dirs, rank, _ = w_gate2.shape
    pairs = dk // 4

    def dup(a):
        a = a.reshape(a.shape[:-1] + (GLA_HEADS, 2, pairs))
        a = jnp.concatenate([a, a], axis=-1)
        return a.reshape(a.shape[:-3] + (GLA_HEADS * dk,))

    w = dup(w_gate2.astype(F32))
    w2e = jnp.zeros((dirs, LANES, GLA_HEADS * dk), F32)
    for d in range(dirs):
        w2e = w2e.at[d, d * rank:(d + 1) * rank].set(w[d])
    b2e = dup(b_gate.astype(F32))[:, None, :]
    return w2e, b2e


def kernel(x, c, ctx, c_ctx, w_mod, b_mod, w_in, na_rpb, gla_w_gate2, gla_b_gate, gla_norm_g,
           w_br_a, w_br_b, w_out, ln_g, ln_b):
    bsz, s, d = x.shape
    lc = ctx.shape[1]
    key_dim, val_dim = d // 2, d
    dk, dv = key_dim // GLA_HEADS, val_dim // GLA_HEADS
    lyr = 0

    widths = (NA_WIDTH,) * 4 + (key_dim, key_dim, val_dim, val_dim, N_DIRS * GATE_RANK, d, d)
    offs = [0]
    for w_ in widths:
        offs.append(offs[-1] + w_)
    w_full = w_in[lyr]
    cols = [w_full[:, offs[i]:offs[i + 1]] for i in range(len(widths))]
    aq, ak, av, az, bq, bk, bv, bz, bg, mga, mgb = cols
    w_lin = jnp.concatenate([aq, ak, av, bv], axis=1).astype(BF16)
    w_rot = jnp.concatenate([bq, bk], axis=1).astype(BF16)
    w_silu = jnp.concatenate([bz, az], axis=1).astype(BF16)
    w_sig = jnp.concatenate([mga, mgb], axis=1).astype(BF16)
    w_gate1 = jnp.pad(bg, ((0, 0), (0, LANES - bg.shape[1]))).astype(BF16)
    tn = 1024
    tm = min(1024, s)

    cs = jnp.concatenate([c, c_ctx[None], jnp.zeros((MOD_ROWS - bsz - 1, d), F32)], axis=0)
    mod = _modulation(cs, w_mod[lyr], b_mod[lyr][None])
    mod_lat = mod[:bsz].reshape(bsz, 3, d)
    mod_ctx = jnp.broadcast_to(mod[bsz].reshape(1, 3, d), (bsz, 3, d))

    na_scale = dict(scaled_tiles=NA_WIDTH // tn, scale=NA_HEAD_DIM ** -0.5)
    rot_scale = dict(scaled_tiles=key_dim // tn, scale=dk ** -0.5)
    p_lin, g_lr, h = _ln_projection(x, mod_lat, w_lin, w_gate1, tm, tn, **na_scale)
    p_rot = _act_projection(h, w_rot, "rope", tm, tn, tables=_rope_tables(s), **rot_scale)
    p_silu = _act_projection(h, w_silu, "silu", tm, tn)
    p_sig = _act_projection(h, w_sig, "sigmoid", tm, tn)
    p_lin_c, g_lr_c, h_c = _ln_projection(ctx, mod_ctx, w_lin, w_gate1, lc, tn, **na_scale)
    p_rot_c = _act_projection(h_c, w_rot, "rope", lc, tn, tables=_identity_rope_tables(lc), **rot_scale)

    y_a = _neighbourhood_attention(p_lin, p_lin_c, _na_bias_table(na_rpb[lyr]))

    w2e, b2e = _expand_gate(gla_w_gate2[lyr], gla_b_gate[lyr], dk)
    w2e = w2e.astype(BF16)
    blocks = dict(q_blk=0, k_blk=key_dim // dk, v_blk=3 * NA_WIDTH // dv, dk=dk, dv=dv)
    s_f = _gla_scan(p_rot_c, p_lin_c, g_lr_c, w2e, b2e, direction=0, tb=lc, emit_o=False, emit_state=True,
                    **blocks)
    s_b = _gla_scan(p_rot_c, p_lin_c, g_lr_c, w2e, b2e, direction=1, tb=lc, emit_o=False, emit_state=True,
                    **blocks)
    tb = min(2048, s)
    o_b =_gla_scan(p_rot, p_lin, g_lr, w2e, b2e, direction=1, tb=tb, s0=s_b, **blocks)
    y_b = _gla_scan(p_rot, p_lin, g_lr, w2e, b2e, direction=0, tb=tb, s0=s_f, ob=o_b,
                    norm_g=gla_norm_g[lyr][None].astype(F32), **blocks)

    return _merge(y_a, y_b, p_silu, p_sig, x, mod_lat,
                  w_br_a[lyr].astype(BF16), w_br_b[lyr].astype(BF16), w_out[lyr].astype(BF16),
                  ln_g[lyr][None], ln_b[lyr][None],
                  az_blk=val_dim // NA_WIDTH, bz_blk=0, ga_blk=0, gb_blk=1)
```

```python
import functools

import jax
import jax.numpy as jnp
from jax import lax
from jax.experimental import pallas as pl
from jax.experimental.pallas import tpu as pltpu

F32 = jnp.float32
BF16 = jnp.bfloat16

GRID_W = 64
WIN_H = 8
WIN_W = 16
NA_HEADS = 16
NA_HEAD_DIM = 64
NA_WIDTH = NA_HEADS * NA_HEAD_DIM
GLA_HEADS = 4
GATE_RANK = 16
GATE_NORMALIZER = 16.0
GLA_CHUNK = 64
GLA_BLOCK = 4 * GLA_CHUNK
N_DIRS = 2
ROPE_BASE = 10000.0
DEPTH = 1
DEEPNORM_ALPHA = (2 * DEPTH) ** 0.25
LN_EPS = 1e-6
RMS_EPS = 1e-6
MASK_VALUE = -1e30

LANES = 128
MOD_ROWS = 8
VMEM_LIMIT = 56 * 2 ** 20


def _params(sem, vmem=VMEM_LIMIT):
    return pltpu.CompilerParams(dimension_semantics=sem, vmem_limit_bytes=vmem)


def _mod_kernel(c_ref, w_ref, b_ref, o_ref):
    c = c_ref[...]
    s = c * jax.nn.sigmoid(c)
    o_ref[...] = jnp.dot(s, w_ref[...], preferred_element_type=F32) + b_ref[...]


def _modulation(cs, w, b, tn=768):
    rows, d = cs.shape
    n = w.shape[1]
    return pl.pallas_call(
        _mod_kernel,
        grid=(n // tn,),
        in_specs=[pl.BlockSpec((rows, d), lambda j: (0, 0)),
                  pl.BlockSpec((d, tn), lambda j: (0, j)),
                  pl.BlockSpec((1, tn), lambda j: (0, j))],
        out_specs=pl.BlockSpec((rows, tn), lambda j: (0, j)),
        out_shape=jax.ShapeDtypeStruct((rows, n), F32),
        compiler_params=_params(("parallel",)),
        name="mod",
    )(cs, w, b)


def _ln_proj_kernel(x_ref, mod_ref, w_ref, wg_ref, p_ref, g_ref, h_ref, *, ln_rows, scaled_tiles, scale):
    n = pl.program_id(2)
    tm = x_ref.shape[1]

    @pl.when(n == 0)
    def _():
        shift = mod_ref[0, 0:1, :]
        scale1 = 1.0 + mod_ref[0, 1:2, :]
        for r0 in range(0, tm, ln_rows):
            xs = x_ref[0, r0:r0 + ln_rows, :]
            mu = jnp.mean(xs, axis=-1, keepdims=True)
            xc = xs - mu
            var = jnp.mean(xc * xc, axis=-1, keepdims=True)
            hb = (xc * lax.rsqrt(var + LN_EPS) * scale1 + shift).astype(BF16)
            h_ref[0, r0:r0 + ln_rows, :] = hb
            g_ref[0, r0:r0 + ln_rows, :] = jnp.dot(hb, wg_ref[...], preferred_element_type=F32)

    acc = jnp.dot(h_ref[0], w_ref[...], preferred_element_type=F32)
    p_ref[0] = (acc * jnp.where(n < scaled_tiles, scale, 1.0)).astype(p_ref.dtype)


def _ln_projection(xin, mod, w, wg, tm, tn, scaled_tiles, scale, ln_rows=256):
    b, t, d = xin.shape
    n = w.shape[1]
    kern = functools.partial(_ln_proj_kernel, ln_rows=min(ln_rows, tm), scaled_tiles=scaled_tiles, scale=scale)
    return pl.pallas_call(
        kern,
        grid=(b, t // tm, n // tn),
        in_specs=[pl.BlockSpec((1, tm, d), lambda bi, i, j: (bi, i, 0)),
                  pl.BlockSpec((1, 3, d), lambda bi, i, j: (bi, 0, 0)),
                  pl.BlockSpec((d, tn), lambda bi, i, j: (0, j)),
                  pl.BlockSpec((d, LANES), lambda bi, i, j: (0, 0))],
        out_specs=[pl.BlockSpec((1, tm, tn), lambda bi, i, j: (bi, i, j)),
                   pl.BlockSpec((1, tm, LANES), lambda bi, i, j: (bi, i, 0)),
                   pl.BlockSpec((1, tm, d), lambda bi, i, j: (bi, i, 0))],
        out_shape=[jax.ShapeDtypeStruct((b, t, n), BF16),
                   jax.ShapeDtypeStruct((b, t, LANES), F32),
                   jax.ShapeDtypeStruct((b, t, d), BF16)],
        compiler_params=_params(("parallel", "parallel", "arbitrary")),
        name="ln_proj",
    )(xin, mod, w, wg)


def _act_proj_kernel(*refs, kind, scaled_tiles, scale):
    if kind == "rope":
        h_ref, w_ref, rowtab_ref, coltab_ref, o_ref = refs
    else:
        h_ref, w_ref, o_ref = refs
    acc = jnp.dot(h_ref[0], w_ref[...], preferred_element_type=F32)
    if kind == "silu":
        half = 0.5 * acc
        o_ref[0] = (half * jnp.tanh(half) + half).astype(o_ref.dtype)
    elif kind == "sigmoid":
        o_ref[0] = (0.5 * jnp.tanh(0.5 * acc) + 0.5).astype(o_ref.dtype)
    else:
        acc = acc * jnp.where(pl.program_id(2) < scaled_tiles, scale, 1.0)
        for r in range(acc.shape[0] // GRID_W):
            rows = slice(r * GRID_W, (r + 1) * GRID_W)
            for j in range(acc.shape[1] // LANES):
                lanes = slice(j * LANES, (j + 1) * LANES)
                u = acc[rows, lanes]
                if j % 2 == 0:
                    c, s = rowtab_ref[r:r + 1, :LANES], rowtab_ref[r:r + 1, LANES:]
                else:
                    c, s = coltab_ref[:, :LANES], coltab_ref[:, LANES:]
                o_ref[0, rows, lanes] = (u * c + pltpu.roll(u, LANES // 2, 1) * s).astype(o_ref.dtype)


def _act_projection(h, w, kind, tm, tn, tables=(), scaled_tiles=0, scale=1.0):
    b, t, d = h.shape
    n = w.shape[1]
    kern = functools.partial(_act_proj_kernel, kind=kind, scaled_tiles=scaled_tiles, scale=scale)
    in_specs = [pl.BlockSpec((1, tm, d), lambda bi, i, j: (bi, i, 0)),
                pl.BlockSpec((d, tn), lambda bi, i, j: (0, j))]
    if tables:
        in_specs += [pl.BlockSpec((tm // GRID_W, 2 * LANES), lambda bi, i, j: (i, 0)),
                     pl.BlockSpec((GRID_W, 2 * LANES), lambda bi, i, j: (0, 0))]
    return pl.pallas_call(
        kern,
        grid=(b, t // tm, n // tn),
        in_specs=in_specs,
        out_specs=pl.BlockSpec((1, tm, tn), lambda bi, i, j: (bi, i, j)),
        out_shape=jax.ShapeDtypeStruct((b, t, n), BF16),
        compiler_params=_params(("parallel", "parallel", "arbitrary")),
        name="proj_" + kind,
    )(h, w, *tables)


def _na_kernel(q_ref, k_ref, v_ref, kc_ref, vc_ref, bias_ref, o_ref, s_buf, p_buf, l_buf, *, rb, rows):
    blk = pl.program_id(2)
    hd = NA_HEAD_DIM
    n_loc = WIN_H * GRID_W
    is_a = lax.broadcasted_iota(jnp.int32, (GRID_W, 2 * hd), 1) < hd
    contract_last = (((1,), (1,)), ((), ()))

    def window_start(i):
        r = blk * rb + i
        r_start = jnp.clip(r - WIN_H // 2, 0, rows - WIN_H)
        return r_start - r + (WIN_H - 1), pl.multiple_of(r_start * GRID_W, GRID_W)

    def scores(i):
        cls, k0 = window_start(i)
        q = q_ref[0, i * GRID_W:(i + 1) * GRID_W, :]
        zero = jnp.zeros_like(q)
        qs = jnp.concatenate([jnp.where(is_a, q, zero), jnp.where(is_a, zero, q)], axis=0)
        k8 = k_ref[0, pl.ds(k0, n_loc), :]
        bias = jnp.concatenate([bias_ref[0, cls + 2 * jj] for jj in range(WIN_H // 2)], axis=1)
        s_buf[i % 2, :, :n_loc] = lax.dot_general(qs, k8, contract_last, preferred_element_type=F32) + bias
        s_buf[i % 2, :, n_loc:] = lax.dot_general(qs, kc_ref[0], contract_last, preferred_element_type=F32)

    def softmax(i):
        s = s_buf[i % 2]
        m = jnp.max(s, axis=-1, keepdims=True)
        p = jnp.exp(s - m)
        l_buf[i % 2] = jnp.sum(p, axis=-1, keepdims=True)
        p_buf[i % 2] = p.astype(BF16)

    def values(i):
        _, k0 = window_start(i)
        v8 = v_ref[0, pl.ds(k0, n_loc), :]
        o = (jnp.dot(p_buf[i % 2, :, :n_loc], v8, preferred_element_type=F32)
             + jnp.dot(p_buf[i % 2, :, n_loc:], vc_ref[0], preferred_element_type=F32))
        o = o / l_buf[i % 2]
        out = jnp.where(is_a, o[:GRID_W], o[GRID_W:])
        o_ref[0, i * GRID_W:(i + 1) * GRID_W, :] = out.astype(o_ref.dtype)

    for t in range(rb + 2):
        if t >= 2:
            values(t - 2)
        if 1 <= t <= rb:
            softmax(t - 1)
        if t < rb:
            scores(t)


def _neighbourhood_attention(p, p_c, bias, rb=32):
    b, s, _ = p.shape
    lc = p_c.shape[1]
    rows = s // GRID_W
    rb = min(rb, rows)
    hg = NA_WIDTH // LANES
    kern = functools.partial(_na_kernel, rb=rb, rows=rows)
    n_keys = WIN_H * GRID_W + lc
    return pl.pallas_call(
        kern,
        grid=(b, hg, rows // rb),
        in_specs=[pl.BlockSpec((1, rb * GRID_W, LANES), lambda bi, g, i: (bi, i, g)),
                  pl.BlockSpec((1, s, LANES), lambda bi, g, i: (bi, 0, hg + g)),
                  pl.BlockSpec((1, s, LANES), lambda bi, g, i: (bi, 0, 2 * hg + g)),
                  pl.BlockSpec((1, lc, LANES), lambda bi, g, i: (bi, 0, hg + g)),
                  pl.BlockSpec((1, lc, LANES), lambda bi, g, i: (bi, 0, 2 * hg + g)),
                  pl.BlockSpec((1, 2 * WIN_H - 2, 2 * GRID_W, 2 * GRID_W), lambda bi, g, i: (g, 0, 0, 0))],
        out_specs=pl.BlockSpec((1, rb * GRID_W, LANES), lambda bi, g, i: (bi, i, g)),
        out_shape=jax.ShapeDtypeStruct((b, s, NA_WIDTH), BF16),
        scratch_shapes=[pltpu.VMEM((2, 2 * GRID_W, n_keys), F32),
                        pltpu.VMEM((2, 2 * GRID_W, n_keys), BF16),
                        pltpu.VMEM((2, 2 * GRID_W, 1), F32)],
        compiler_params=_params(("parallel", "parallel", "arbitrary")),
        name="na",
    )(p, p, p, p_c, p_c, bias)


def _na_bias_table(rpb):
    qcol = jnp.arange(GRID_W)[:, None]
    kcol = jnp.arange(GRID_W)[None, :]
    c_start = jnp.clip(qcol - WIN_W // 2, 0, GRID_W - WIN_W)
    in_win = (kcol >= c_start) & (kcol < c_start + WIN_W)
    dc = jnp.clip(kcol - qcol + (WIN_W - 1), 0, 2 * WIN_W - 2)
    band = jnp.where(in_win[None, None], rpb[:, :, dc], MASK_VALUE).astype(F32)
    n_dr = band.shape[1]
    band = band.reshape(NA_HEADS // 2, 2, n_dr, GRID_W, GRID_W).transpose(0, 2, 1, 3, 4)
    band = band.reshape(NA_HEADS // 2, n_dr, 2 * GRID_W, GRID_W)
    return jnp.concatenate([band[:, :-1], band[:, 1:]], axis=-1)


def _gla_kernel(*refs, reverse, has_init, emit_o, combine, emit_state):
    refs = list(refs)
    q_ref, k_ref, v_ref, lr_ref, w2_ref, b2_ref = refs[:6]
    pos = 6
    s0_ref = ob_ref = ng_ref = out_ref = sfin_ref = None
    if has_init:
        s0_ref = refs[pos]; pos += 1
    if combine:
        ob_ref, ng_ref = refs[pos], refs[pos + 1]; pos += 2
    if emit_o:
        out_ref = refs[pos]; pos += 1
    if emit_state:
        sfin_ref = refs[pos]; pos += 1
    s_ref, cum_buf, kend_buf, dec_buf, att_buf, qin_buf = refs[pos:pos + 6]

    j = pl.program_id(2)
    nblk = pl.num_programs(2)
    blk_len = GLA_BLOCK
    half = blk_len // 2
    c_len = GLA_CHUNK

    @pl.when(j == 0)
    def _():
        if has_init:
            s_ref[...] = s0_ref[0, 0]
        else:
            s_ref[...] = jnp.zeros_like(s_ref)

    ri = lax.broadcasted_iota(jnp.int32, (blk_len, blk_len), 0)
    ci = lax.broadcasted_iota(jnp.int32, (blk_len, blk_len), 1)
    tri = (ci >= ri) if reverse else (ci <= ri)
    tri_b = tri.astype(BF16)
    tri_h = tri[:half, :half]
    contract_last = (((1,), (1,)), ((), ()))
    contract_first = (((0,), (0,)), ((), ()))

    def bcast_halves(lo_row, hi_row):
        return jnp.concatenate([jnp.broadcast_to(lo_row, (half, lo_row.shape[1])),
                                jnp.broadcast_to(hi_row, (half, hi_row.shape[1]))], axis=0)

    def decay(r0):
        rows = slice(r0, r0 + blk_len)
        z = jnp.dot(lr_ref[0, rows, :].astype(BF16), w2_ref[0], preferred_element_type=F32) + b2_ref[0]
        g = (jnp.minimum(z, 0.0) - jnp.log(1.0 + jnp.exp(-jnp.abs(z)))) * (1.0 / GATE_NORMALIZER)
        g_hi = g.astype(BF16)
        g_lo = (g - g_hi.astype(F32)).astype(BF16)
        cum_buf[rows, :] = (jnp.dot(tri_b, g_hi, preferred_element_type=F32)
                            + jnp.dot(tri_b, g_lo, preferred_element_type=F32))

    def prepare(r0, slot):
        rows = slice(r0, r0 + blk_len)
        cum = cum_buf[rows, :]

        def row(i):
            return cum_buf[r0 + i:r0 + i + 1, :]

        if reverse:
            m_lo, m_hi, mid, end = row(c_len), row(half + c_len), row(half), row(0)
        else:
            m_lo, m_hi, mid, end = row(c_len - 1), row(half + c_len - 1), row(half - 1), row(blk_len - 1)
        m_ref = bcast_halves(m_lo, m_hi)
        def expb(a):
            return jnp.exp(a).astype(BF16)

        km = k_ref[0, rows, :] * expb(m_ref - cum)
        kend_buf[slot] = km * bcast_halves(expb(end - m_lo), expb(end - m_hi))
        dec_buf[slot] = jnp.exp(end)
        if emit_o:
            qm = q_ref[0, rows, :] * expb(cum - m_ref)
            a_lo = lax.dot_general(qm[:half], km[:half], contract_last, preferred_element_type=F32)
            a_hi = lax.dot_general(qm[half:], km[half:], contract_last, preferred_element_type=F32)
            a_lo = jnp.where(tri_h, a_lo, 0.0)
            a_hi = jnp.where(tri_h, a_hi, 0.0)
            zeros = jnp.zeros((half, half), F32)
            if reverse:
                a_off = lax.dot_general(qm[:half] * expb(m_lo - mid), km[half:] * expb(mid - m_hi),
                                        contract_last, preferred_element_type=F32)
                att = jnp.concatenate([jnp.concatenate([a_lo, a_off], axis=1),
                                       jnp.concatenate([zeros, a_hi], axis=1)], axis=0)
            else:
                a_off = lax.dot_general(qm[half:] * expb(m_hi - mid), km[:half] * expb(mid - m_lo),
                                        contract_last, preferred_element_type=F32)
                att = jnp.concatenate([jnp.concatenate([a_lo, zeros], axis=1),
                                       jnp.concatenate([a_off, a_hi], axis=1)], axis=0)
            att_buf[slot] = att.astype(BF16)
            qin_buf[slot] = qm * bcast_halves(expb(m_lo), expb(m_hi))

    def advance(r0, slot):
        rows = slice(r0, r0 + blk_len)
        v = v_ref[0, rows, :]
        st = s_ref[...]
        if emit_o:
            o = (jnp.dot(att_buf[slot], v, preferred_element_type=F32)
                 + lax.dot_general(qin_buf[slot], st.astype(BF16), contract_last, preferred_element_type=F32))
            if combine:
                tot = o + ob_ref[0, rows, :].astype(F32)
                ms = jnp.mean(tot * tot, axis=-1, keepdims=True)
                o = tot * lax.rsqrt(ms + RMS_EPS) * ng_ref[...]
            out_ref[0, rows, :] = o.astype(out_ref.dtype)
        s_ref[...] = st * dec_buf[slot] + lax.dot_general(v, kend_buf[slot], contract_first,
                                                          preferred_element_type=F32)

    nb = q_ref.shape[1] // blk_len
    order = list(range(nb - 1, -1, -1) if reverse else range(nb))
    for bi in order:
        decay(bi * blk_len)
    prepare(order[0] * blk_len, 0)
    for n, bi in enumerate(order):
        if n + 1 < nb:
            prepare(order[n + 1] * blk_len, (n + 1) % 2)
        advance(bi * blk_len, n % 2)

    if emit_state:
        @pl.when(j == nblk - 1)
        def _():
            sfin_ref[0, 0] = s_ref[...]


def _gla_scan(p_qk, p_v, g_lr, w2e, b2e, *, direction, tb, q_blk, k_blk, v_blk, dk, dv,
              s0=None, ob=None, norm_g=None, emit_o=True, emit_state=False):
    b, t, _ = p_qk.shape
    nblk = t // tb
    reverse = direction == 1
    combine = ob is not None
    has_init = s0 is not None

    def blk(j):
        return (nblk - 1 - j) if reverse else j

    in_specs = [pl.BlockSpec((1, tb, dk), lambda bi, h, j: (bi, blk(j), q_blk + h)),
                pl.BlockSpec((1, tb, dk), lambda bi, h, j: (bi, blk(j), k_blk + h)),
                pl.BlockSpec((1, tb, dv), lambda bi, h, j: (bi, blk(j), v_blk + h)),
                pl.BlockSpec((1, tb, LANES), lambda bi, h, j: (bi, blk(j), 0)),
                pl.BlockSpec((1, LANES, dk), lambda bi, h, j: (direction, 0, h)),
                pl.BlockSpec((1, 1, dk), lambda bi, h, j: (direction, 0, h))]
    args = [p_qk, p_qk, p_v, g_lr, w2e, b2e]
    if has_init:
        in_specs.append(pl.BlockSpec((1, 1, dv, dk), lambda bi, h, j: (bi, h, 0, 0)))
        args.append(s0)
    if combine:
        in_specs.append(pl.BlockSpec((1, tb, dv), lambda bi, h, j: (bi, blk(j), h)))
        in_specs.append(pl.BlockSpec((1, dv), lambda bi, h, j: (0, 0)))
        args += [ob, norm_g]
    out_specs, out_shape = [], []
    if emit_o:
        out_specs.append(pl.BlockSpec((1, tb, dv), lambda bi, h, j: (bi, blk(j), h)))
        out_shape.append(jax.ShapeDtypeStruct((b, t, GLA_HEADS * dv), BF16))
    if emit_state:
        out_specs.append(pl.BlockSpec((1, 1, dv, dk), lambda bi, h, j: (bi, h, 0, 0)))
        out_shape.append(jax.ShapeDtypeStruct((b, GLA_HEADS, dv, dk), F32))
    assert tb % GLA_BLOCK == 0
    kern = functools.partial(_gla_kernel, reverse=reverse, has_init=has_init,
                             emit_o=emit_o, combine=combine, emit_state=emit_state)
    res = pl.pallas_call(
        kern,
        grid=(b, GLA_HEADS, nblk),
        in_specs=in_specs,
        out_specs=out_specs,
        out_shape=out_shape,
        scratch_shapes=[pltpu.VMEM((dv, dk), F32),
                        pltpu.VMEM((tb, dk), F32),
                        pltpu.VMEM((2, GLA_BLOCK, dk), BF16),
                        pltpu.VMEM((2, 1, dk), F32),
                        pltpu.VMEM((2, GLA_BLOCK, GLA_BLOCK), BF16),
                        pltpu.VMEM((2, GLA_BLOCK, dk), BF16)],
        compiler_params=_params(("parallel", "parallel", "arbitrary")),
        name="gla_" + ("bwd" if reverse else "fwd") + ("_ctx" if not emit_o else ""),
    )(*args)
    return res[0] if len(res) == 1 else res


def _merge_kernel(ya_ref, az_ref, yb_ref, bz_ref, ga_ref, gb_ref, x_ref, mod_ref,
                  wa_ref, wb_ref, wo_ref, lng_ref, lnb_ref, o_ref):
    ua = (ya_ref[0].astype(F32) * az_ref[0].astype(F32)).astype(BF16)
    ub = (yb_ref[0].astype(F32) * bz_ref[0].astype(F32)).astype(BF16)
    pa = jnp.dot(ua, wa_ref[...], preferred_element_type=F32)
    pb = jnp.dot(ub, wb_ref[...], preferred_element_type=F32)
    m = (ga_ref[0].astype(F32) * pa + gb_ref[0].astype(F32) * pb).astype(BF16)
    out = jnp.dot(m, wo_ref[...], preferred_element_type=F32)
    z = DEEPNORM_ALPHA * x_ref[0] + mod_ref[0, 2:3, :] * out
    mu = jnp.mean(z, axis=-1, keepdims=True)
    zc = z - mu
    var = jnp.mean(zc * zc, axis=-1, keepdims=True)
    o_ref[0] = zc * lax.rsqrt(var + LN_EPS) * lng_ref[...] + lnb_ref[...]


def _merge(y_a, y_b, p_silu, p_gate, x, mod, wa, wb, wo, ln_g, ln_b, az_blk, bz_blk, ga_blk, gb_blk, tm=256):
    b, s, d = x.shape
    na = y_a.shape[2]
    dv = y_b.shape[2]
    const = dict(pipeline_mode=pl.Buffered(1))
    return pl.pallas_call(
        _merge_kernel,
        grid=(b, s // tm),
        in_specs=[pl.BlockSpec((1, tm, na), lambda bi, i: (bi, i, 0)),
                  pl.BlockSpec((1, tm, na), lambda bi, i: (bi, i, az_blk)),
                  pl.BlockSpec((1, tm, dv), lambda bi, i: (bi, i, 0)),
                  pl.BlockSpec((1, tm, dv), lambda bi, i: (bi, i, bz_blk)),
                  pl.BlockSpec((1, tm, d), lambda bi, i: (bi, i, ga_blk)),
                  pl.BlockSpec((1, tm, d), lambda bi, i: (bi, i, gb_blk)),
                  pl.BlockSpec((1, tm, d), lambda bi, i: (bi, i, 0)),
                  pl.BlockSpec((1, 3, d), lambda bi, i: (bi, 0, 0)),
                  pl.BlockSpec((na, d), lambda bi, i: (0, 0), **const),
                  pl.BlockSpec((dv, d), lambda bi, i: (0, 0), **const),
                  pl.BlockSpec((d, d), lambda bi, i: (0, 0), **const),
                  pl.BlockSpec((1, d), lambda bi, i: (0, 0)),
                  pl.BlockSpec((1, d), lambda bi, i: (0, 0))],
        out_specs=pl.BlockSpec((1, tm, d), lambda bi, i: (bi, i, 0)),
        out_shape=jax.ShapeDtypeStruct((b, s, d), F32),
        compiler_params=_params(("parallel", "parallel")),
        name="merge",
    )(y_a, p_silu, y_b, p_silu, p_gate, p_gate, x, mod, wa, wb, wo, ln_g, ln_b)


def _rope_tables(s):
    pairs = LANES // 2
    inv_freq = ROPE_BASE ** (-jnp.arange(pairs, dtype=F32) / pairs)

    def table(n):
        ang = jnp.arange(n).astype(F32)[:, None] * inv_freq[None, :]
        return jnp.concatenate([jnp.cos(ang), jnp.cos(ang), -jnp.sin(ang), jnp.sin(ang)], axis=-1)

    return table(s // GRID_W), table(GRID_W)


def _identity_rope_tables(s):
    def table(n):
        return jnp.concatenate([jnp.ones((n, LANES), F32), jnp.zeros((n, LANES), F32)], axis=-1)

    return table(s // GRID_W), table(GRID_W)


def _expand_gate(w_gate2, b_gate, dk):
    dirs, rank, _ = w_gate2.shape
    pairs = dk // 4

    def dup(a):
        a = a.reshape(a.shape[:-1] + (GLA_HEADS, 2, pairs))
        a = jnp.concatenate([a, a], axis=-1)
        return a.reshape(a.shape[:-3] + (GLA_HEADS * dk,))

    w = dup(w_gate2.astype(F32))
    w2e = jnp.zeros((dirs, LANES, GLA_HEADS * dk), F32)
    for d in range(dirs):
        w2e = w2e.at[d, d * rank:(d + 1) * rank].set(w[d])
    b2e = dup(b_gate.astype(F32))[:, None, :]
    return w2e, b2e


def kernel(x, c, ctx, c_ctx, w_mod, b_mod, w_in, na_rpb, gla_w_gate2, gla_b_gate, gla_norm_g,
           w_br_a, w_br_b, w_out, ln_g, ln_b):
    bsz, s, d = x.shape
    lc = ctx.shape[1]
    key_dim, val_dim = d // 2, d
    dk, dv = key_dim // GLA_HEADS, val_dim // GLA_HEADS
    lyr = 0

    widths = (NA_WIDTH,) * 4 + (key_dim, key_dim, val_dim, val_dim, N_DIRS * GATE_RANK, d, d)
    offs = [0]
    for w_ in widths:
        offs.append(offs[-1] + w_)
    w_full = w_in[lyr]
    cols = [w_full[:, offs[i]:offs[i + 1]] for i in range(len(widths))]
    aq, ak, av, az, bq, bk, bv, bz, bg, mga, mgb = cols
    w_lin = jnp.concatenate([aq, ak, av, bv], axis=1).astype(BF16)
    w_rot = jnp.concatenate([bq, bk], axis=1).astype(BF16)
    w_silu = jnp.concatenate([bz, az], axis=1).astype(BF16)
    w_sig = jnp.concatenate([mga, mgb], axis=1).astype(BF16)
    w_gate1 = jnp.pad(bg, ((0, 0), (0, LANES - bg.shape[1]))).astype(BF16)
    tn = 1024
    tm = min(1024, s)

    cs = jnp.concatenate([c, c_ctx[None], jnp.zeros((MOD_ROWS - bsz - 1, d), F32)], axis=0)
    mod = _modulation(cs, w_mod[lyr], b_mod[lyr][None])
    mod_lat = mod[:bsz].reshape(bsz, 3, d)
    mod_ctx = jnp.broadcast_to(mod[bsz].reshape(1, 3, d), (bsz, 3, d))

    na_scale = dict(scaled_tiles=NA_WIDTH // tn, scale=NA_HEAD_DIM ** -0.5)
    rot_scale = dict(scaled_tiles=key_dim // tn, scale=dk ** -0.5)
    p_lin, g_lr, h = _ln_projection(x, mod_lat, w_lin, w_gate1, tm, tn, **na_scale)
    tm_act = min(2048, s)
    p_rot = _act_projection(h, w_rot, "rope", tm_act, tn, tables=_rope_tables(s), **rot_scale)
    p_silu = _act_projection(h, w_silu, "silu", tm_act, tn)
    p_sig = _act_projection(h, w_sig, "sigmoid", tm_act, tn)
    p_lin_c, g_lr_c, h_c = _ln_projection(ctx, mod_ctx, w_lin, w_gate1, lc, tn, **na_scale)
    p_rot_c = _act_projection(h_c, w_rot, "rope", lc, tn, tables=_identity_rope_tables(lc), **rot_scale)

    y_a = _neighbourhood_attention(p_lin, p_lin_c, _na_bias_table(na_rpb[lyr]))

    w2e, b2e = _expand_gate(gla_w_gate2[lyr], gla_b_gate[lyr], dk)
    w2e = w2e.astype(BF16)
    blocks = dict(q_blk=0, k_blk=key_dim // dk, v_blk=3 * NA_WIDTH // dv, dk=dk, dv=dv)
    s_f = _gla_scan(p_rot_c, p_lin_c, g_lr_c, w2e, b2e, direction=0, tb=lc, emit_o=False, emit_state=True,
                    **blocks)
    s_b = _gla_scan(p_rot_c, p_lin_c, g_lr_c, w2e, b2e, direction=1, tb=lc, emit_o=False, emit_state=True,
                    **blocks)
    tb = min(2048, s)
    o_b =_gla_scan(p_rot, p_lin, g_lr, w2e, b2e, direction=1, tb=tb, s0=s_b, **blocks)
    y_b = _gla_scan(p_rot, p_lin, g_lr, w2e, b2e, direction=0, tb=tb, s0=s_f, ob=o_b,
                    norm_g=gla_norm_g[lyr][None].astype(F32), **blocks)

    return _merge(y_a, y_b, p_silu, p_sig, x, mod_lat,
                  w_br_a[lyr].astype(BF16), w_br_b[lyr].astype(BF16), w_out[lyr].astype(BF16),
                  ln_g[lyr][None], ln_b[lyr][None],
                  az_blk=val_dim // NA_WIDTH, bz_blk=0, ga_blk=0, gb_blk=1)
```

```python
import functools

import jax
import jax.numpy as jnp
from jax import lax
from jax.experimental import pallas as pl
from jax.experimental.pallas import tpu as pltpu

F32 = jnp.float32
BF16 = jnp.bfloat16

GRID_W = 64
WIN_H = 8
WIN_W = 16
NA_HEADS = 16
NA_HEAD_DIM = 64
NA_WIDTH = NA_HEADS * NA_HEAD_DIM
GLA_HEADS = 4
GATE_RANK = 16
GATE_NORMALIZER = 16.0
GLA_CHUNK = 64
GLA_BLOCK = 4 * GLA_CHUNK
N_DIRS = 2
ROPE_BASE = 10000.0
DEPTH = 1
DEEPNORM_ALPHA = (2 * DEPTH) ** 0.25
LN_EPS = 1e-6
RMS_EPS = 1e-6
MASK_VALUE = -1e30

LANES = 128
MOD_ROWS = 8
VMEM_LIMIT = 56 * 2 ** 20


def _params(sem, vmem=VMEM_LIMIT):
    return pltpu.CompilerParams(dimension_semantics=sem, vmem_limit_bytes=vmem)


def _mod_kernel(c_ref, w_ref, b_ref, o_ref):
    c = c_ref[...]
    s = c * jax.nn.sigmoid(c)
    o_ref[...] = jnp.dot(s, w_ref[...], preferred_element_type=F32) + b_ref[...]


def _modulation(cs, w, b, tn=768):
    rows, d = cs.shape
    n = w.shape[1]
    return pl.pallas_call(
        _mod_kernel,
        grid=(n // tn,),
        in_specs=[pl.BlockSpec((rows, d), lambda j: (0, 0)),
                  pl.BlockSpec((d, tn), lambda j: (0, j)),
                  pl.BlockSpec((1, tn), lambda j: (0, j))],
        out_specs=pl.BlockSpec((rows, tn), lambda j: (0, j)),
        out_shape=jax.ShapeDtypeStruct((rows, n), F32),
        compiler_params=_params(("parallel",)),
        name="mod",
    )(cs, w, b)


def _ln_proj_kernel(x_ref, mod_ref, w_ref, wg_ref, p_ref, g_ref, h_ref, *, ln_rows, scaled_tiles, scale):
    n = pl.program_id(2)
    tm = x_ref.shape[1]
    col_scale = jnp.where(n < scaled_tiles, scale, 1.0)

    @pl.when(n == 0)
    def _():
        shift = mod_ref[0, 0:1, :]
        scale1 = 1.0 + mod_ref[0, 1:2, :]
        for r0 in range(0, tm, ln_rows):
            rows = slice(r0, r0 + ln_rows)
            xs = x_ref[0, rows, :]
            mu = jnp.mean(xs, axis=-1, keepdims=True)
            xc = xs - mu
            var = jnp.mean(xc * xc, axis=-1, keepdims=True)
            hb = (xc * lax.rsqrt(var + LN_EPS) * scale1 + shift).astype(BF16)
            h_ref[0, rows, :] = hb
            g_ref[0, rows, :] = jnp.dot(hb, wg_ref[...], preferred_element_type=F32)
            p_ref[0, rows, :] = (jnp.dot(hb, w_ref[...], preferred_element_type=F32) * col_scale).astype(p_ref.dtype)

    @pl.when(n > 0)
    def _():
        acc = jnp.dot(h_ref[0], w_ref[...], preferred_element_type=F32)
        p_ref[0] = (acc * col_scale).astype(p_ref.dtype)


def _ln_projection(xin, mod, w, wg, tm, tn, scaled_tiles, scale, ln_rows=256):
    b, t, d = xin.shape
    n = w.shape[1]
    kern = functools.partial(_ln_proj_kernel, ln_rows=min(ln_rows, tm), scaled_tiles=scaled_tiles, scale=scale)
    return pl.pallas_call(
        kern,
        grid=(b, t // tm, n // tn),
        in_specs=[pl.BlockSpec((1, tm, d), lambda bi, i, j: (bi, i, 0)),
                  pl.BlockSpec((1, 3, d), lambda bi, i, j: (bi, 0, 0)),
                  pl.BlockSpec((d, tn), lambda bi, i, j: (0, j)),
                  pl.BlockSpec((d, LANES), lambda bi, i, j: (0, 0))],
        out_specs=[pl.BlockSpec((1, tm, tn), lambda bi, i, j: (bi, i, j)),
                   pl.BlockSpec((1, tm, LANES), lambda bi, i, j: (bi, i, 0)),
                   pl.BlockSpec((1, tm, d), lambda bi, i, j: (bi, i, 0))],
        out_shape=[jax.ShapeDtypeStruct((b, t, n), BF16),
                   jax.ShapeDtypeStruct((b, t, LANES), F32),
                   jax.ShapeDtypeStruct((b, t, d), BF16)],
        compiler_params=_params(("parallel", "parallel", "arbitrary")),
        name="ln_proj",
    )(xin, mod, w, wg)


def _act_proj_kernel(*refs, kind, scaled_tiles, scale):
    if kind == "rope":
        h_ref, w_ref, rowtab_ref, coltab_ref, o_ref = refs
    else:
        h_ref, w_ref, o_ref = refs
    acc = jnp.dot(h_ref[0], w_ref[...], preferred_element_type=F32)
    if kind == "silu":
        half = 0.5 * acc
        o_ref[0] = (half * jnp.tanh(half) + half).astype(o_ref.dtype)
    elif kind == "sigmoid":
        o_ref[0] = (0.5 * jnp.tanh(0.5 * acc) + 0.5).astype(o_ref.dtype)
    else:
        acc = acc * jnp.where(pl.program_id(2) < scaled_tiles, scale, 1.0)
        for r in range(acc.shape[0] // GRID_W):
            rows = slice(r * GRID_W, (r + 1) * GRID_W)
            for j in range(acc.shape[1] // LANES):
                lanes = slice(j * LANES, (j + 1) * LANES)
                u = acc[rows, lanes]
                if j % 2 == 0:
                    c, s = rowtab_ref[r:r + 1, :LANES], rowtab_ref[r:r + 1, LANES:]
                else:
                    c, s = coltab_ref[:, :LANES], coltab_ref[:, LANES:]
                o_ref[0, rows, lanes] = (u * c + pltpu.roll(u, LANES // 2, 1) * s).astype(o_ref.dtype)


def _act_projection(h, w, kind, tm, tn, tables=(), scaled_tiles=0, scale=1.0):
    b, t, d = h.shape
    n = w.shape[1]
    kern = functools.partial(_act_proj_kernel, kind=kind, scaled_tiles=scaled_tiles, scale=scale)
    in_specs = [pl.BlockSpec((1, tm, d), lambda bi, i, j: (bi, i, 0)),
                pl.BlockSpec((d, tn), lambda bi, i, j: (0, j))]
    if tables:
        in_specs += [pl.BlockSpec((tm // GRID_W, 2 * LANES), lambda bi, i, j: (i, 0)),
                     pl.BlockSpec((GRID_W, 2 * LANES), lambda bi, i, j: (0, 0))]
    return pl.pallas_call(
        kern,
        grid=(b, t // tm, n // tn),
        in_specs=in_specs,
        out_specs=pl.BlockSpec((1, tm, tn), lambda bi, i, j: (bi, i, j)),
        out_shape=jax.ShapeDtypeStruct((b, t, n), BF16),
        compiler_params=_params(("parallel", "parallel", "arbitrary")),
        name="proj_" + kind,
    )(h, w, *tables)


def _na_kernel(q_ref, k_ref, v_ref, kc_ref, vc_ref, bias_ref, o_ref, s_buf, p_buf, l_buf, *, rb, rows):
    blk = pl.program_id(2)
    hd = NA_HEAD_DIM
    n_loc = WIN_H * GRID_W
    is_a = lax.broadcasted_iota(jnp.int32, (GRID_W, 2 * hd), 1) < hd
    contract_last = (((1,), (1,)), ((), ()))

    def window_start(i):
        r = blk * rb + i
        r_start = jnp.clip(r - WIN_H // 2, 0, rows - WIN_H)
        return r_start - r + (WIN_H - 1), pl.multiple_of(r_start * GRID_W, GRID_W)

    def scores(i):
        cls, k0 = window_start(i)
        q = q_ref[0, i * GRID_W:(i + 1) * GRID_W, :]
        zero = jnp.zeros_like(q)
        qs = jnp.concatenate([jnp.where(is_a, q, zero), jnp.where(is_a, zero, q)], axis=0)
        k8 = k_ref[0, pl.ds(k0, n_loc), :]
        bias = jnp.concatenate([bias_ref[0, cls + 2 * jj] for jj in range(WIN_H // 2)], axis=1)
        s_buf[i % 2, :, :n_loc] = lax.dot_general(qs, k8, contract_last, preferred_element_type=F32) + bias
        s_buf[i % 2, :, n_loc:] = lax.dot_general(qs, kc_ref[0], contract_last, preferred_element_type=F32)

    def softmax(i):
        s = s_buf[i % 2]
        m = jnp.max(s, axis=-1, keepdims=True)
        p = jnp.exp(s - m)
        l_buf[i % 2] = jnp.sum(p, axis=-1, keepdims=True)
        p_buf[i % 2] = p.astype(BF16)

    def values(i):
        _, k0 = window_start(i)
        v8 = v_ref[0, pl.ds(k0, n_loc), :]
        o = (jnp.dot(p_buf[i % 2, :, :n_loc], v8, preferred_element_type=F32)
             + jnp.dot(p_buf[i % 2, :, n_loc:], vc_ref[0], preferred_element_type=F32))
        o = o / l_buf[i % 2]
        out = jnp.where(is_a, o[:GRID_W], o[GRID_W:])
        o_ref[0, i * GRID_W:(i + 1) * GRID_W, :] = out.astype(o_ref.dtype)

    for t in range(rb + 2):
        if t >= 2:
            values(t - 2)
        if 1 <= t <= rb:
            softmax(t - 1)
        if t < rb:
            scores(t)


def _neighbourhood_attention(p, p_c, bias, rb=32):
    b, s, _ = p.shape
    lc = p_c.shape[1]
    rows = s // GRID_W
    rb = min(rb, rows)
    hg = NA_WIDTH // LANES
    kern = functools.partial(_na_kernel, rb=rb, rows=rows)
    n_keys = WIN_H * GRID_W + lc
    return pl.pallas_call(
        kern,
        grid=(b, hg, rows // rb),
        in_specs=[pl.BlockSpec((1, rb * GRID_W, LANES), lambda bi, g, i: (bi, i, g)),
                  pl.BlockSpec((1, s, LANES), lambda bi, g, i: (bi, 0, hg + g)),
                  pl.BlockSpec((1, s, LANES), lambda bi, g, i: (bi, 0, 2 * hg + g)),
                  pl.BlockSpec((1, lc, LANES), lambda bi, g, i: (bi, 0, hg + g)),
                  pl.BlockSpec((1, lc, LANES), lambda bi, g, i: (bi, 0, 2 * hg + g)),
                  pl.BlockSpec((1, 2 * WIN_H - 2, 2 * GRID_W, 2 * GRID_W), lambda bi, g, i: (g, 0, 0, 0))],
        out_specs=pl.BlockSpec((1, rb * GRID_W, LANES), lambda bi, g, i: (bi, i, g)),
        out_shape=jax.ShapeDtypeStruct((b, s, NA_WIDTH), BF16),
        scratch_shapes=[pltpu.VMEM((2, 2 * GRID_W, n_keys), F32),
                        pltpu.VMEM((2, 2 * GRID_W, n_keys), BF16),
                        pltpu.VMEM((2, 2 * GRID_W, 1), F32)],
        compiler_params=_params(("parallel", "parallel", "arbitrary")),
        name="na",
    )(p, p, p, p_c, p_c, bias)


def _na_bias_table(rpb):
    qcol = jnp.arange(GRID_W)[:, None]
    kcol = jnp.arange(GRID_W)[None, :]
    c_start = jnp.clip(qcol - WIN_W // 2, 0, GRID_W - WIN_W)
    in_win = (kcol >= c_start) & (kcol < c_start + WIN_W)
    ring = 2 * GRID_W
    fill = jnp.full(rpb.shape[:2] + (ring - (2 * WIN_W - 1),), MASK_VALUE, rpb.dtype)
    ext = jnp.concatenate([rpb[:, :, WIN_W - 1:], fill, rpb[:, :, :WIN_W - 1]], axis=-1)
    toep = jnp.tile(ext, (1, 1, GRID_W))[:, :, :GRID_W * (ring - 1)]
    toep = toep.reshape(rpb.shape[:2] + (GRID_W, ring - 1))[..., :GRID_W]
    band = jnp.where(in_win[None, None], toep, MASK_VALUE).astype(F32)
    n_dr = band.shape[1]
    band = band.reshape(NA_HEADS // 2, 2, n_dr, GRID_W, GRID_W).transpose(0, 2, 1, 3, 4)
    band = band.reshape(NA_HEADS // 2, n_dr, 2 * GRID_W, GRID_W)
    return jnp.concatenate([band[:, :-1], band[:, 1:]], axis=-1)


def _gla_kernel(*refs, reverse, has_init, emit_o, combine, emit_state):
    refs = list(refs)
    q_ref, k_ref, v_ref, lr_ref, w2_ref, b2_ref = refs[:6]
    pos = 6
    s0_ref = ob_ref = ng_ref = out_ref = sfin_ref = None
    if has_init:
        s0_ref = refs[pos]; pos += 1
    if combine:
        ob_ref, ng_ref = refs[pos], refs[pos + 1]; pos += 2
    if emit_o:
        out_ref = refs[pos]; pos += 1
    if emit_state:
        sfin_ref = refs[pos]; pos += 1
    s_ref, cum_buf, kend_buf, dec_buf, att_buf, qin_buf = refs[pos:pos + 6]

    j = pl.program_id(2)
    nblk = pl.num_programs(2)
    blk_len = GLA_BLOCK
    half = blk_len // 2
    c_len = GLA_CHUNK

    @pl.when(j == 0)
    def _():
        if has_init:
            s_ref[...] = s0_ref[0, 0]
        else:
            s_ref[...] = jnp.zeros_like(s_ref)

    ri = lax.broadcasted_iota(jnp.int32, (blk_len, blk_len), 0)
    ci = lax.broadcasted_iota(jnp.int32, (blk_len, blk_len), 1)
    tri = (ci >= ri) if reverse else (ci <= ri)
    tri_b = tri.astype(BF16)
    tri_h = tri[:half, :half]
    contract_last = (((1,), (1,)), ((), ()))
    contract_first = (((0,), (0,)), ((), ()))

    def bcast_halves(lo_row, hi_row):
        return jnp.concatenate([jnp.broadcast_to(lo_row, (half, lo_row.shape[1])),
                                jnp.broadcast_to(hi_row, (half, hi_row.shape[1]))], axis=0)

    def decay(r0):
        rows = slice(r0, r0 + blk_len)
        z = jnp.dot(lr_ref[0, rows, :].astype(BF16), w2_ref[0], preferred_element_type=F32) + b2_ref[0]
        g = (jnp.minimum(z, 0.0) - jnp.log(1.0 + jnp.exp(-jnp.abs(z)))) * (1.0 / GATE_NORMALIZER)
        g_hi = g.astype(BF16)
        g_lo = (g - g_hi.astype(F32)).astype(BF16)
        cum_buf[rows, :] = (jnp.dot(tri_b, g_hi, preferred_element_type=F32)
                            + jnp.dot(tri_b, g_lo, preferred_element_type=F32))

    def prepare(r0, slot):
        rows = slice(r0, r0 + blk_len)
        cum = cum_buf[rows, :]

        def row(i):
            return cum_buf[r0 + i:r0 + i + 1, :]

        if reverse:
            m_lo, m_hi, mid, end = row(c_len), row(half + c_len), row(half), row(0)
        else:
            m_lo, m_hi, mid, end = row(c_len - 1), row(half + c_len - 1), row(half - 1), row(blk_len - 1)
        m_ref = bcast_halves(m_lo, m_hi)
        def expb(a):
            return jnp.exp(a).astype(BF16)

        km = k_ref[0, rows, :] * expb(m_ref - cum)
        kend_buf[slot] = km * bcast_halves(expb(end - m_lo), expb(end - m_hi))
        dec_buf[slot] = jnp.exp(end)
        if emit_o:
            qm = q_ref[0, rows, :] * expb(cum - m_ref)
            a_lo = lax.dot_general(qm[:half], km[:half], contract_last, preferred_element_type=F32)
            a_hi = lax.dot_general(qm[half:], km[half:], contract_last, preferred_element_type=F32)
            a_lo = jnp.where(tri_h, a_lo, 0.0)
            a_hi = jnp.where(tri_h, a_hi, 0.0)
            zeros = jnp.zeros((half, half), F32)
            if reverse:
                a_off = lax.dot_general(qm[:half] * expb(m_lo - mid), km[half:] * expb(mid - m_hi),
                                        contract_last, preferred_element_type=F32)
                att = jnp.concatenate([jnp.concatenate([a_lo, a_off], axis=1),
                                       jnp.concatenate([zeros, a_hi], axis=1)], axis=0)
            else:
                a_off = lax.dot_general(qm[half:] * expb(m_hi - mid), km[:half] * expb(mid - m_lo),
                                        contract_last, preferred_element_type=F32)
                att = jnp.concatenate([jnp.concatenate([a_lo, zeros], axis=1),
                                       jnp.concatenate([a_off, a_hi], axis=1)], axis=0)
            att_buf[slot] = att.astype(BF16)
            qin_buf[slot] = qm * bcast_halves(expb(m_lo), expb(m_hi))

    def advance(r0, slot):
        rows = slice(r0, r0 + blk_len)
        v = v_ref[0, rows, :]
        st = s_ref[...]
        if emit_o:
            o = (jnp.dot(att_buf[slot], v, preferred_element_type=F32)
                 + lax.dot_general(qin_buf[slot], st.astype(BF16), contract_last, preferred_element_type=F32))
            if combine:
                tot = o + ob_ref[0, rows, :].astype(F32)
                ms = jnp.mean(tot * tot, axis=-1, keepdims=True)
                o = tot * lax.rsqrt(ms + RMS_EPS) * ng_ref[...]
            out_ref[0, rows, :] = o.astype(out_ref.dtype)
        s_ref[...] = st * dec_buf[slot] + lax.dot_general(v, kend_buf[slot], contract_first,
                                                          preferred_element_type=F32)

    nb = q_ref.shape[1] // blk_len
    order = list(range(nb - 1, -1, -1) if reverse else range(nb))
    for bi in order:
        decay(bi * blk_len)
    prepare(order[0] * blk_len, 0)
    for n, bi in enumerate(order):
        if n + 1 < nb:
            prepare(order[n + 1] * blk_len, (n + 1) % 2)
        advance(bi * blk_len, n % 2)

    if emit_state:
        @pl.when(j == nblk - 1)
        def _():
            sfin_ref[0, 0] = s_ref[...]


def _gla_scan(p_qk, p_v, g_lr, w2e, b2e, *, direction, tb, q_blk, k_blk, v_blk, dk, dv,
              s0=None, ob=None, norm_g=None, emit_o=True, emit_state=False):
    b, t, _ = p_qk.shape
    nblk = t // tb
    reverse = direction == 1
    combine = ob is not None
    has_init = s0 is not None

    def blk(j):
        return (nblk - 1 - j) if reverse else j

    in_specs = [pl.BlockSpec((1, tb, dk), lambda bi, h, j: (bi, blk(j), q_blk + h)),
                pl.BlockSpec((1, tb, dk), lambda bi, h, j: (bi, blk(j), k_blk + h)),
                pl.BlockSpec((1, tb, dv), lambda bi, h, j: (bi, blk(j), v_blk + h)),
                pl.BlockSpec((1, tb, LANES), lambda bi, h, j: (bi, blk(j), 0)),
                pl.BlockSpec((1, LANES, dk), lambda bi, h, j: (direction, 0, h)),
                pl.BlockSpec((1, 1, dk), lambda bi, h, j: (direction, 0, h))]
    args = [p_qk, p_qk, p_v, g_lr, w2e, b2e]
    if has_init:
        in_specs.append(pl.BlockSpec((1, 1, dv, dk), lambda bi, h, j: (bi, h, 0, 0)))
        args.append(s0)
    if combine:
        in_specs.append(pl.BlockSpec((1, tb, dv), lambda bi, h, j: (bi, blk(j), h)))
        in_specs.append(pl.BlockSpec((1, dv), lambda bi, h, j: (0, 0)))
        args += [ob, norm_g]
    out_specs, out_shape = [], []
    if emit_o:
        out_specs.append(pl.BlockSpec((1, tb, dv), lambda bi, h, j: (bi, blk(j), h)))
        out_shape.append(jax.ShapeDtypeStruct((b, t, GLA_HEADS * dv), BF16))
    if emit_state:
        out_specs.append(pl.BlockSpec((1, 1, dv, dk), lambda bi, h, j: (bi, h, 0, 0)))
        out_shape.append(jax.ShapeDtypeStruct((b, GLA_HEADS, dv, dk), F32))
    assert tb % GLA_BLOCK == 0
    kern = functools.partial(_gla_kernel, reverse=reverse, has_init=has_init,
                             emit_o=emit_o, combine=combine, emit_state=emit_state)
    res = pl.pallas_call(
        kern,
        grid=(b, GLA_HEADS, nblk),
        in_specs=in_specs,
        out_specs=out_specs,
        out_shape=out_shape,
        scratch_shapes=[pltpu.VMEM((dv, dk), F32),
                        pltpu.VMEM((tb, dk), F32),
                        pltpu.VMEM((2, GLA_BLOCK, dk), BF16),
                        pltpu.VMEM((2, 1, dk), F32),
                        pltpu.VMEM((2, GLA_BLOCK, GLA_BLOCK), BF16),
                        pltpu.VMEM((2, GLA_BLOCK, dk), BF16)],
        compiler_params=_params(("parallel", "parallel", "arbitrary")),
        name="gla_" + ("bwd" if reverse else "fwd") + ("_ctx" if not emit_o else ""),
    )(*args)
    return res[0] if len(res) == 1 else res


def _branch_merge_kernel(ya_ref, az_ref, yb_ref, bz_ref, ga_ref, gb_ref, wa_ref, wb_ref, m_ref, *, chunk):
    for r0 in range(0, ya_ref.shape[1], chunk):
        rows = slice(r0, r0 + chunk)
        ua = (ya_ref[0, rows, :].astype(F32) * az_ref[0, rows, :].astype(F32)).astype(BF16)
        ub = (yb_ref[0, rows, :].astype(F32) * bz_ref[0, rows, :].astype(F32)).astype(BF16)
        pa = jnp.dot(ua, wa_ref[...], preferred_element_type=F32)
        pb = jnp.dot(ub, wb_ref[...], preferred_element_type=F32)
        m_ref[0, rows, :] = (ga_ref[0, rows, :].astype(F32) * pa
                             + gb_ref[0, rows, :].astype(F32) * pb).astype(m_ref.dtype)


def _output_kernel(m_ref, x_ref, mod_ref, wo_ref, lng_ref, lnb_ref, o_ref, *, chunk):
    for r0 in range(0, m_ref.shape[1], chunk):
        rows = slice(r0, r0 + chunk)
        out = jnp.dot(m_ref[0, rows, :], wo_ref[...], preferred_element_type=F32)
        z = DEEPNORM_ALPHA * x_ref[0, rows, :] + mod_ref[0, 2:3, :] * out
        mu = jnp.mean(z, axis=-1, keepdims=True)
        zc = z - mu
        var = jnp.mean(zc * zc, axis=-1, keepdims=True)
        o_ref[0, rows, :] = zc * lax.rsqrt(var + LN_EPS) * lng_ref[...] + lnb_ref[...]


def _merge(y_a, y_b, p_silu, p_gate, x, mod, wa, wb, wo, ln_g, ln_b, az_blk, bz_blk, ga_blk, gb_blk,
           tm=512, chunk=256):
    b, s, d = x.shape
    na = y_a.shape[2]
    dv = y_b.shape[2]
    tm = min(tm, s)
    const = dict(pipeline_mode=pl.Buffered(1))
    merged = pl.pallas_call(
        functools.partial(_branch_merge_kernel, chunk=chunk),
        grid=(b, s // tm),
        in_specs=[pl.BlockSpec((1, tm, na), lambda bi, i: (bi, i, 0)),
                  pl.BlockSpec((1, tm, na), lambda bi, i: (bi, i, az_blk)),
                  pl.BlockSpec((1, tm, dv), lambda bi, i: (bi, i, 0)),
                  pl.BlockSpec((1, tm, dv), lambda bi, i: (bi, i, bz_blk)),
                  pl.BlockSpec((1, tm, d), lambda bi, i: (bi, i, ga_blk)),
                  pl.BlockSpec((1, tm, d), lambda bi, i: (bi, i, gb_blk)),
                  pl.BlockSpec((na, d), lambda bi, i: (0, 0), **const),
                  pl.BlockSpec((dv, d), lambda bi, i: (0, 0), **const)],
        out_specs=pl.BlockSpec((1, tm, d), lambda bi, i: (bi, i, 0)),
        out_shape=jax.ShapeDtypeStruct((b, s, d), BF16),
        compiler_params=_params(("parallel", "parallel")),
        name="branch_merge",
    )(y_a, p_silu, y_b, p_silu, p_gate, p_gate, wa, wb)
    return pl.pallas_call(
        functools.partial(_output_kernel, chunk=chunk // 2),
        grid=(b, s // tm),
        in_specs=[pl.BlockSpec((1, tm, d), lambda bi, i: (bi, i, 0)),
                  pl.BlockSpec((1, tm, d), lambda bi, i: (bi, i, 0)),
                  pl.BlockSpec((1, 3, d), lambda bi, i: (bi, 0, 0)),
                  pl.BlockSpec((d, d), lambda bi, i: (0, 0), **const),
                  pl.BlockSpec((1, d), lambda bi, i: (0, 0)),
                  pl.BlockSpec((1, d), lambda bi, i: (0, 0))],
        out_specs=pl.BlockSpec((1, tm, d), lambda bi, i: (bi, i, 0)),
        out_shape=jax.ShapeDtypeStruct((b, s, d), F32),
        compiler_params=_params(("parallel", "parallel")),
        name="output",
    )(merged, x, mod, wo, ln_g, ln_b)


def _rope_tables(s):
    pairs = LANES // 2
    inv_freq = ROPE_BASE ** (-jnp.arange(pairs, dtype=F32) / pairs)

    def table(n):
        ang = jnp.arange(n).astype(F32)[:, None] * inv_freq[None, :]
        return jnp.concatenate([jnp.cos(ang), jnp.cos(ang), -jnp.sin(ang), jnp.sin(ang)], axis=-1)

    return table(s // GRID_W), table(GRID_W)


def _identity_rope_tables(s):
    def table(n):
        return jnp.concatenate([jnp.ones((n, LANES), F32), jnp.zeros((n, LANES), F32)], axis=-1)

    return table(s // GRID_W), table(GRID_W)


def _expand_gate(w_gate2, b_gate, dk):
    dirs, rank, _ = w_gate2.shape
    pairs = dk // 4

    def dup(a):
        a = a.reshape(a.shape[:-1] + (GLA_HEADS, 2, pairs))
        a = jnp.concatenate([a, a], axis=-1)
        return a.reshape(a.shape[:-3] + (GLA_HEADS * dk,))

    w = dup(w_gate2.astype(F32))
    w2e = jnp.zeros((dirs, LANES, GLA_HEADS * dk), F32)
    for d in range(dirs):
        w2e = w2e.at[d, d * rank:(d + 1) * rank].set(w[d])
    b2e = dup(b_gate.astype(F32))[:, None, :]
    return w2e, b2e


def kernel(x, c, ctx, c_ctx, w_mod, b_mod, w_in, na_rpb, gla_w_gate2, gla_b_gate, gla_norm_g,
           w_br_a, w_br_b, w_out, ln_g, ln_b):
    bsz, s, d = x.shape
    lc = ctx.shape[1]
    key_dim, val_dim = d // 2, d
    dk, dv = key_dim // GLA_HEADS, val_dim // GLA_HEADS
    lyr = 0

    widths = (NA_WIDTH,) * 4 + (key_dim, key_dim, val_dim, val_dim, N_DIRS * GATE_RANK, d, d)
    offs = [0]
    for w_ in widths:
        offs.append(offs[-1] + w_)
    w_full = w_in[lyr]
    cols = [w_full[:, offs[i]:offs[i + 1]] for i in range(len(widths))]
    aq, ak, av, az, bq, bk, bv, bz, bg, mga, mgb = cols
    w_lin = jnp.concatenate([aq, ak, av, bv], axis=1).astype(BF16)
    w_rot = jnp.concatenate([bq, bk], axis=1).astype(BF16)
    w_silu = jnp.concatenate([bz, az], axis=1).astype(BF16)
    w_sig = jnp.concatenate([mga, mgb], axis=1).astype(BF16)
    w_gate1 = jnp.pad(bg, ((0, 0), (0, LANES - bg.shape[1]))).astype(BF16)
    tn = 1024
    tm = min(1024, s)

    cs = jnp.concatenate([c, c_ctx[None], jnp.zeros((MOD_ROWS - bsz - 1, d), F32)], axis=0)
    mod = _modulation(cs, w_mod[lyr], b_mod[lyr][None])
    mod_lat = mod[:bsz].reshape(bsz, 3, d)
    mod_ctx = jnp.broadcast_to(mod[bsz].reshape(1, 3, d), (bsz, 3, d))

    na_scale = dict(scaled_tiles=NA_WIDTH // tn, scale=NA_HEAD_DIM ** -0.5)
    rot_scale = dict(scaled_tiles=key_dim // tn, scale=dk ** -0.5)
    p_lin, g_lr, h = _ln_projection(x, mod_lat, w_lin, w_gate1, tm, tn, **na_scale)
    tm_act = min(2048, s)
    p_rot = _act_projection(h, w_rot, "rope", tm_act, tn, tables=_rope_tables(s), **rot_scale)
    p_silu = _act_projection(h, w_silu, "silu", tm_act, tn)
    p_sig = _act_projection(h, w_sig, "sigmoid", tm_act, tn)
    p_lin_c, g_lr_c, h_c = _ln_projection(ctx, mod_ctx, w_lin, w_gate1, lc, tn, **na_scale)
    p_rot_c = _act_projection(h_c, w_rot, "rope", lc, tn, tables=_identity_rope_tables(lc), **rot_scale)

    y_a = _neighbourhood_attention(p_lin, p_lin_c, _na_bias_table(na_rpb[lyr]))

    w2e, b2e = _expand_gate(gla_w_gate2[lyr], gla_b_gate[lyr], dk)
    w2e = w2e.astype(BF16)
    blocks = dict(q_blk=0, k_blk=key_dim // dk, v_blk=3 * NA_WIDTH // dv, dk=dk, dv=dv)
    s_f = _gla_scan(p_rot_c, p_lin_c, g_lr_c, w2e, b2e, direction=0, tb=lc, emit_o=False, emit_state=True,
                    **blocks)
    s_b = _gla_scan(p_rot_c, p_lin_c, g_lr_c, w2e, b2e, direction=1, tb=lc, emit_o=False, emit_state=True,
                    **blocks)
    tb = min(2048, s)
    o_b =_gla_scan(p_rot, p_lin, g_lr, w2e, b2e, direction=1, tb=tb, s0=s_b, **blocks)
    y_b = _gla_scan(p_rot, p_lin, g_lr, w2e, b2e, direction=0, tb=tb, s0=s_f, ob=o_b,
                    norm_g=gla_norm_g[lyr][None].astype(F32), **blocks)

    return _merge(y_a, y_b, p_silu, p_sig, x, mod_lat,
                  w_br_a[lyr].astype(BF16), w_br_b[lyr].astype(BF16), w_out[lyr].astype(BF16),
                  ln_g[lyr][None], ln_b[lyr][None],
                  az_blk=val_dim // NA_WIDTH, bz_blk=0, ga_blk=0, gb_blk=1)
```

```python
import functools

import jax
import jax.numpy as jnp
from jax import lax
from jax.experimental import pallas as pl
from jax.experimental.pallas import tpu as pltpu

F32 = jnp.float32
BF16 = jnp.bfloat16

GRID_W = 64
WIN_H = 8
WIN_W = 16
NA_HEADS = 16
NA_HEAD_DIM = 64
NA_WIDTH = NA_HEADS * NA_HEAD_DIM
GLA_HEADS = 4
GATE_RANK = 16
GATE_NORMALIZER = 16.0
GLA_CHUNK = 64
GLA_BLOCK = 4 * GLA_CHUNK
N_DIRS = 2
ROPE_BASE = 10000.0
DEPTH = 1
DEEPNORM_ALPHA = (2 * DEPTH) ** 0.25
LN_EPS = 1e-6
RMS_EPS = 1e-6
MASK_VALUE = -1e30

LANES = 128
MOD_ROWS = 8
VMEM_LIMIT = 56 * 2 ** 20


def _params(sem, vmem=VMEM_LIMIT):
    return pltpu.CompilerParams(dimension_semantics=sem, vmem_limit_bytes=vmem)


def _mod_kernel(c_ref, w_ref, b_ref, o_ref):
    c = c_ref[...]
    s = c * jax.nn.sigmoid(c)
    o_ref[...] = jnp.dot(s, w_ref[...], preferred_element_type=F32) + b_ref[...]


def _modulation(cs, w, b, tn=768):
    rows, d = cs.shape
    n = w.shape[1]
    return pl.pallas_call(
        _mod_kernel,
        grid=(n // tn,),
        in_specs=[pl.BlockSpec((rows, d), lambda j: (0, 0)),
                  pl.BlockSpec((d, tn), lambda j: (0, j)),
                  pl.BlockSpec((1, tn), lambda j: (0, j))],
        out_specs=pl.BlockSpec((rows, tn), lambda j: (0, j)),
        out_shape=jax.ShapeDtypeStruct((rows, n), F32),
        compiler_params=_params(("parallel",)),
        name="mod",
    )(cs, w, b)


def _ln_proj_kernel(x_ref, mod_ref, w_ref, wg_ref, p_ref, g_ref, h_ref, *, ln_rows, scaled_tiles, scale):
    n = pl.program_id(2)
    tm = x_ref.shape[1]
    col_scale = jnp.where(n < scaled_tiles, scale, 1.0)

    @pl.when(n == 0)
    def _():
        shift = mod_ref[0, 0:1, :]
        scale1 = 1.0 + mod_ref[0, 1:2, :]
        for r0 in range(0, tm, ln_rows):
            rows = slice(r0, r0 + ln_rows)
            xs = x_ref[0, rows, :]
            mu = jnp.mean(xs, axis=-1, keepdims=True)
            xc = xs - mu
            var = jnp.mean(xc * xc, axis=-1, keepdims=True)
            hb = (xc * lax.rsqrt(var + LN_EPS) * scale1 + shift).astype(BF16)
            h_ref[0, rows, :] = hb
            g_ref[0, rows, :] = jnp.dot(hb, wg_ref[...], preferred_element_type=F32)
            p_ref[0, rows, :] = (jnp.dot(hb, w_ref[...], preferred_element_type=F32) * col_scale).astype(p_ref.dtype)

    @pl.when(n > 0)
    def _():
        acc = jnp.dot(h_ref[0], w_ref[...], preferred_element_type=F32)
        p_ref[0] = (acc * col_scale).astype(p_ref.dtype)


def _column_tile_map(col_tiles):
    def index_map(bi, i, j):
        tile = col_tiles[0]
        for pos in range(1, len(col_tiles)):
            tile = jnp.where(j >= pos, col_tiles[pos], tile)
        return (0, tile)
    return index_map


def _ln_projection(xin, mod, w, col_tiles, wg, tm, tn, scaled_tiles, scale, ln_rows=256):
    b, t, d = xin.shape
    n = len(col_tiles) * tn
    kern = functools.partial(_ln_proj_kernel, ln_rows=min(ln_rows, tm), scaled_tiles=scaled_tiles, scale=scale)
    return pl.pallas_call(
        kern,
        grid=(b, t // tm, n // tn),
        in_specs=[pl.BlockSpec((1, tm, d), lambda bi, i, j: (bi, i, 0)),
                  pl.BlockSpec((1, 3, d), lambda bi, i, j: (bi, 0, 0)),
                  pl.BlockSpec((d, tn), _column_tile_map(col_tiles)),
                  pl.BlockSpec((d, LANES), lambda bi, i, j: (0, 0))],
        out_specs=[pl.BlockSpec((1, tm, tn), lambda bi, i, j: (bi, i, j)),
                   pl.BlockSpec((1, tm, LANES), lambda bi, i, j: (bi, i, 0)),
                   pl.BlockSpec((1, tm, d), lambda bi, i, j: (bi, i, 0))],
        out_shape=[jax.ShapeDtypeStruct((b, t, n), BF16),
                   jax.ShapeDtypeStruct((b, t, LANES), F32),
                   jax.ShapeDtypeStruct((b, t, d), BF16)],
        compiler_params=_params(("parallel", "parallel", "arbitrary")),
        name="ln_proj",
    )(xin, mod, w, wg)


def _act_proj_kernel(*refs, kind, scaled_tiles, scale):
    if kind == "rope":
        h_ref, w_ref, rowtab_ref, coltab_ref, o_ref = refs
    else:
        h_ref, w_ref, o_ref = refs
    acc = jnp.dot(h_ref[0], w_ref[...], preferred_element_type=F32)
    if kind == "silu":
        half = 0.5 * acc
        o_ref[0] = (half * jnp.tanh(half) + half).astype(o_ref.dtype)
    elif kind == "sigmoid":
        o_ref[0] = (0.5 * jnp.tanh(0.5 * acc) + 0.5).astype(o_ref.dtype)
    else:
        acc = acc * jnp.where(pl.program_id(2) < scaled_tiles, scale, 1.0)
        for r in range(acc.shape[0] // GRID_W):
            rows = slice(r * GRID_W, (r + 1) * GRID_W)
            for j in range(acc.shape[1] // LANES):
                lanes = slice(j * LANES, (j + 1) * LANES)
                u = acc[rows, lanes]
                if j % 2 == 0:
                    c, s = rowtab_ref[r:r + 1, :LANES], rowtab_ref[r:r + 1, LANES:]
                else:
                    c, s = coltab_ref[:, :LANES], coltab_ref[:, LANES:]
                o_ref[0, rows, lanes] = (u * c + pltpu.roll(u, LANES // 2, 1) * s).astype(o_ref.dtype)


def _act_projection(h, w, col_tiles, kind, tm, tn, tables=(), scaled_tiles=0, scale=1.0):
    b, t, d = h.shape
    n = len(col_tiles) * tn
    kern = functools.partial(_act_proj_kernel, kind=kind, scaled_tiles=scaled_tiles, scale=scale)
    in_specs = [pl.BlockSpec((1, tm, d), lambda bi, i, j: (bi, i, 0)),
                pl.BlockSpec((d, tn), _column_tile_map(col_tiles))]
    if tables:
        in_specs += [pl.BlockSpec((tm // GRID_W, 2 * LANES), lambda bi, i, j: (i, 0)),
                     pl.BlockSpec((GRID_W, 2 * LANES), lambda bi, i, j: (0, 0))]
    return pl.pallas_call(
        kern,
        grid=(b, t // tm, n // tn),
        in_specs=in_specs,
        out_specs=pl.BlockSpec((1, tm, tn), lambda bi, i, j: (bi, i, j)),
        out_shape=jax.ShapeDtypeStruct((b, t, n), BF16),
        compiler_params=_params(("parallel", "parallel", "arbitrary")),
        name="proj_" + kind,
    )(h, w, *tables)


def _na_halo_rows(rb, rows):
    return min(rb + WIN_H, rows)


def _na_halo_start(blk, rb, rows):
    return jnp.clip(blk * rb - WIN_H // 2, 0, rows - _na_halo_rows(rb, rows))


def _na_kernel(q_ref, k_ref, v_ref, kc_ref, vc_ref, bias_ref, o_ref, s_buf, p_buf, l_buf, *, rb, rows):
    blk = pl.program_id(2)
    hd = NA_HEAD_DIM
    n_loc = WIN_H * GRID_W
    is_a = lax.broadcasted_iota(jnp.int32, (GRID_W, 2 * hd), 1) < hd
    contract_last = (((1,), (1,)), ((), ()))

    halo_start = _na_halo_start(blk, rb, rows)

    def window_start(i):
        r = blk * rb + i
        r_start = jnp.clip(r - WIN_H // 2, 0, rows - WIN_H)
        return r_start - r + (WIN_H - 1), pl.multiple_of((r_start - halo_start) * GRID_W, GRID_W)

    def scores(i):
        cls, k0 = window_start(i)
        q = q_ref[0, i * GRID_W:(i + 1) * GRID_W, :]
        zero = jnp.zeros_like(q)
        qs = jnp.concatenate([jnp.where(is_a, q, zero), jnp.where(is_a, zero, q)], axis=0)
        k8 = k_ref[0, pl.ds(k0, n_loc), :]
        bias = jnp.concatenate([bias_ref[0, cls + 2 * jj] for jj in range(WIN_H // 2)], axis=1)
        s_buf[i % 2, :, :n_loc] = lax.dot_general(qs, k8, contract_last, preferred_element_type=F32) + bias
        s_buf[i % 2, :, n_loc:] = lax.dot_general(qs, kc_ref[0], contract_last, preferred_element_type=F32)

    def softmax(i):
        s = s_buf[i % 2]
        m = jnp.max(s, axis=-1, keepdims=True)
        p = jnp.exp(s - m)
        l_buf[i % 2] = jnp.sum(p, axis=-1, keepdims=True)
        p_buf[i % 2] = p.astype(BF16)

    def values(i):
        _, k0 = window_start(i)
        v8 = v_ref[0, pl.ds(k0, n_loc), :]
        o = (jnp.dot(p_buf[i % 2, :, :n_loc], v8, preferred_element_type=F32)
             + jnp.dot(p_buf[i % 2, :, n_loc:], vc_ref[0], preferred_element_type=F32))
        o = o / l_buf[i % 2]
        out = jnp.where(is_a, o[:GRID_W], o[GRID_W:])
        o_ref[0, i * GRID_W:(i + 1) * GRID_W, :] = out.astype(o_ref.dtype)

    for t in range(rb + 2):
        if t >= 2:
            values(t - 2)
        if 1 <= t <= rb:
            softmax(t - 1)
        if t < rb:
            scores(t)


def _neighbourhood_attention(p, p_c, bias, rb=32):
    b, s, _ = p.shape
    lc = p_c.shape[1]
    rows = s // GRID_W
    rb = min(rb, rows)
    hg = NA_WIDTH // LANES
    kern = functools.partial(_na_kernel, rb=rb, rows=rows)
    n_keys = WIN_H * GRID_W + lc
    halo_tokens = _na_halo_rows(rb, rows) * GRID_W

    def halo_spec(first_group):
        return pl.BlockSpec(
            (pl.Element(1), pl.Element(halo_tokens), pl.Element(LANES)),
            lambda bi, g, i: (bi, _na_halo_start(i, rb, rows) * GRID_W, (first_group + g) * LANES))

    return pl.pallas_call(
        kern,
        grid=(b, hg, rows // rb),
        in_specs=[pl.BlockSpec((1, rb * GRID_W, LANES), lambda bi, g, i: (bi, i, g)),
                  halo_spec(hg), halo_spec(2 * hg),
                  pl.BlockSpec((1, lc, LANES), lambda bi, g, i: (bi, 0, hg + g)),
                  pl.BlockSpec((1, lc, LANES), lambda bi, g, i: (bi, 0, 2 * hg + g)),
                  pl.BlockSpec((1, 2 * WIN_H - 2, 2 * GRID_W, 2 * GRID_W), lambda bi, g, i: (g, 0, 0, 0))],
        out_specs=pl.BlockSpec((1, rb * GRID_W, LANES), lambda bi, g, i: (bi, i, g)),
        out_shape=jax.ShapeDtypeStruct((b, s, NA_WIDTH), BF16),
        scratch_shapes=[pltpu.VMEM((2, 2 * GRID_W, n_keys), F32),
                        pltpu.VMEM((2, 2 * GRID_W, n_keys), BF16),
                        pltpu.VMEM((2, 2 * GRID_W, 1), F32)],
        compiler_params=_params(("parallel", "parallel", "arbitrary")),
        name="na",
    )(p, p, p, p_c, p_c, bias)


def _na_bias_table(rpb):
    qcol = jnp.arange(GRID_W)[:, None]
    kcol = jnp.arange(GRID_W)[None, :]
    c_start = jnp.clip(qcol - WIN_W // 2, 0, GRID_W - WIN_W)
    in_win = (kcol >= c_start) & (kcol < c_start + WIN_W)
    ring = 2 * GRID_W
    fill = jnp.full(rpb.shape[:2] + (ring - (2 * WIN_W - 1),), MASK_VALUE, rpb.dtype)
    ext = jnp.concatenate([rpb[:, :, WIN_W - 1:], fill, rpb[:, :, :WIN_W - 1]], axis=-1)
    toep = jnp.tile(ext, (1, 1, GRID_W))[:, :, :GRID_W * (ring - 1)]
    toep = toep.reshape(rpb.shape[:2] + (GRID_W, ring - 1))[..., :GRID_W]
    band = jnp.where(in_win[None, None], toep, MASK_VALUE).astype(F32)
    n_dr = band.shape[1]
    band = band.reshape(NA_HEADS // 2, 2, n_dr, GRID_W, GRID_W).transpose(0, 2, 1, 3, 4)
    band = band.reshape(NA_HEADS // 2, n_dr, 2 * GRID_W, GRID_W)
    return jnp.concatenate([band[:, :-1], band[:, 1:]], axis=-1)


def _gla_kernel(*refs, reverse, has_init, emit_o, combine, emit_state):
    refs = list(refs)
    q_ref, k_ref, v_ref, lr_ref, w2_ref, b2_ref = refs[:6]
    pos = 6
    s0_ref = ob_ref = ng_ref = out_ref = sfin_ref = None
    if has_init:
        s0_ref = refs[pos]; pos += 1
    if combine:
        ob_ref, ng_ref = refs[pos], refs[pos + 1]; pos += 2
    if emit_o:
        out_ref = refs[pos]; pos += 1
    if emit_state:
        sfin_ref = refs[pos]; pos += 1
    s_ref, cum_buf, kend_buf, dec_buf, att_buf, qin_buf = refs[pos:pos + 6]

    j = pl.program_id(2)
    nblk = pl.num_programs(2)
    blk_len = GLA_BLOCK
    half = blk_len // 2
    c_len = GLA_CHUNK

    @pl.when(j == 0)
    def _():
        if has_init:
            s_ref[...] = s0_ref[0, 0]
        else:
            s_ref[...] = jnp.zeros_like(s_ref)

    ri = lax.broadcasted_iota(jnp.int32, (blk_len, blk_len), 0)
    ci = lax.broadcasted_iota(jnp.int32, (blk_len, blk_len), 1)
    tri = (ci >= ri) if reverse else (ci <= ri)
    tri_b = tri.astype(BF16)
    tri_h = tri[:half, :half]
    contract_last = (((1,), (1,)), ((), ()))
    contract_first = (((0,), (0,)), ((), ()))

    def bcast_halves(lo_row, hi_row):
        return jnp.concatenate([jnp.broadcast_to(lo_row, (half, lo_row.shape[1])),
                                jnp.broadcast_to(hi_row, (half, hi_row.shape[1]))], axis=0)

    def decay(r0):
        rows = slice(r0, r0 + blk_len)
        z = jnp.dot(lr_ref[0, rows, :].astype(BF16), w2_ref[0], preferred_element_type=F32) + b2_ref[0]
        g = (jnp.minimum(z, 0.0) - jnp.log(1.0 + jnp.exp(-jnp.abs(z)))) * (1.0 / GATE_NORMALIZER)
        g_hi = g.astype(BF16)
        g_lo = (g - g_hi.astype(F32)).astype(BF16)
        cum_buf[rows, :] = (jnp.dot(tri_b, g_hi, preferred_element_type=F32)
                            + jnp.dot(tri_b, g_lo, preferred_element_type=F32))

    def prepare(r0, slot):
        rows = slice(r0, r0 + blk_len)
        cum = cum_buf[rows, :]

        def row(i):
            return cum_buf[r0 + i:r0 + i + 1, :]

        if reverse:
            m_lo, m_hi, mid, end = row(c_len), row(half + c_len), row(half), row(0)
        else:
            m_lo, m_hi, mid, end = row(c_len - 1), row(half + c_len - 1), row(half - 1), row(blk_len - 1)
        m_ref = bcast_halves(m_lo, m_hi)
        def expb(a):
            return jnp.exp(a).astype(BF16)

        km = k_ref[0, rows, :] * expb(m_ref - cum)
        kend_buf[slot] = km * bcast_halves(expb(end - m_lo), expb(end - m_hi))
        dec_buf[slot] = jnp.exp(end)
        if emit_o:
            qm = q_ref[0, rows, :] * expb(cum - m_ref)
            a_lo = lax.dot_general(qm[:half], km[:half], contract_last, preferred_element_type=F32)
            a_hi = lax.dot_general(qm[half:], km[half:], contract_last, preferred_element_type=F32)
            a_lo = jnp.where(tri_h, a_lo, 0.0)
            a_hi = jnp.where(tri_h, a_hi, 0.0)
            zeros = jnp.zeros((half, half), F32)
            if reverse:
                a_off = lax.dot_general(qm[:half] * expb(m_lo - mid), km[half:] * expb(mid - m_hi),
                                        contract_last, preferred_element_type=F32)
                att = jnp.concatenate([jnp.concatenate([a_lo, a_off], axis=1),
                                       jnp.concatenate([zeros, a_hi], axis=1)], axis=0)
            else:
                a_off = lax.dot_general(qm[half:] * expb(m_hi - mid), km[:half] * expb(mid - m_lo),
                                        contract_last, preferred_element_type=F32)
                att = jnp.concatenate([jnp.concatenate([a_lo, zeros], axis=1),
                                       jnp.concatenate([a_off, a_hi], axis=1)], axis=0)
            att_buf[slot] = att.astype(BF16)
            qin_buf[slot] = qm * bcast_halves(expb(m_lo), expb(m_hi))

    def advance(r0, slot):
        rows = slice(r0, r0 + blk_len)
        v = v_ref[0, rows, :]
        st = s_ref[...]
        if emit_o:
            o = (jnp.dot(att_buf[slot], v, preferred_element_type=F32)
                 + lax.dot_general(qin_buf[slot], st.astype(BF16), contract_last, preferred_element_type=F32))
            if combine:
                tot = o + ob_ref[0, rows, :].astype(F32)
                ms = jnp.mean(tot * tot, axis=-1, keepdims=True)
                o = tot * lax.rsqrt(ms + RMS_EPS) * ng_ref[...]
            out_ref[0, rows, :] = o.astype(out_ref.dtype)
        s_ref[...] = st * dec_buf[slot] + lax.dot_general(v, kend_buf[slot], contract_first,
                                                          preferred_element_type=F32)

    nb = q_ref.shape[1] // blk_len
    order = list(range(nb - 1, -1, -1) if reverse else range(nb))
    for bi in order:
        decay(bi * blk_len)
    prepare(order[0] * blk_len, 0)
    for n, bi in enumerate(order):
        if n + 1 < nb:
            prepare(order[n + 1] * blk_len, (n + 1) % 2)
        advance(bi * blk_len, n % 2)

    if emit_state:
        @pl.when(j == nblk - 1)
        def _():
            sfin_ref[0, 0] = s_ref[...]


def _gla_scan(p_qk, p_v, g_lr, w2e, b2e, *, direction, tb, q_blk, k_blk, v_blk, dk, dv,
              s0=None, ob=None, norm_g=None, emit_o=True, emit_state=False):
    b, t, _ = p_qk.shape
    nblk = t // tb
    reverse = direction == 1
    combine = ob is not None
    has_init = s0 is not None

    def blk(j):
        return (nblk - 1 - j) if reverse else j

    in_specs = [pl.BlockSpec((1, tb, dk), lambda bi, h, j: (bi, blk(j), q_blk + h)),
                pl.BlockSpec((1, tb, dk), lambda bi, h, j: (bi, blk(j), k_blk + h)),
                pl.BlockSpec((1, tb, dv), lambda bi, h, j: (bi, blk(j), v_blk + h)),
                pl.BlockSpec((1, tb, LANES), lambda bi, h, j: (bi, blk(j), 0)),
                pl.BlockSpec((1, LANES, dk), lambda bi, h, j: (direction, 0, h)),
                pl.BlockSpec((1, 1, dk), lambda bi, h, j: (direction, 0, h))]
    args = [p_qk, p_qk, p_v, g_lr, w2e, b2e]
    if has_init:
        in_specs.append(pl.BlockSpec((1, 1, dv, dk), lambda bi, h, j: (bi, h, 0, 0)))
        args.append(s0)
    if combine:
        in_specs.append(pl.BlockSpec((1, tb, dv), lambda bi, h, j: (bi, blk(j), h)))
        in_specs.append(pl.BlockSpec((1, dv), lambda bi, h, j: (0, 0)))
        args += [ob, norm_g]
    out_specs, out_shape = [], []
    if emit_o:
        out_specs.append(pl.BlockSpec((1, tb, dv), lambda bi, h, j: (bi, blk(j), h)))
        out_shape.append(jax.ShapeDtypeStruct((b, t, GLA_HEADS * dv), BF16))
    if emit_state:
        out_specs.append(pl.BlockSpec((1, 1, dv, dk), lambda bi, h, j: (bi, h, 0, 0)))
        out_shape.append(jax.ShapeDtypeStruct((b, GLA_HEADS, dv, dk), F32))
    assert tb % GLA_BLOCK == 0
    kern = functools.partial(_gla_kernel, reverse=reverse, has_init=has_init,
                             emit_o=emit_o, combine=combine, emit_state=emit_state)
    res = pl.pallas_call(
        kern,
        grid=(b, GLA_HEADS, nblk),
        in_specs=in_specs,
        out_specs=out_specs,
        out_shape=out_shape,
        scratch_shapes=[pltpu.VMEM((dv, dk), F32),
                        pltpu.VMEM((tb, dk), F32),
                        pltpu.VMEM((2, GLA_BLOCK, dk), BF16),
                        pltpu.VMEM((2, 1, dk), F32),
                        pltpu.VMEM((2, GLA_BLOCK, GLA_BLOCK), BF16),
                        pltpu.VMEM((2, GLA_BLOCK, dk), BF16)],
        compiler_params=_params(("parallel", "parallel", "arbitrary")),
        name="gla_" + ("bwd" if reverse else "fwd") + ("_ctx" if not emit_o else ""),
    )(*args)
    return res[0] if len(res) == 1 else res


def _branch_merge_kernel(ya_ref, az_ref, yb_ref, bz_ref, ga_ref, gb_ref, wa_ref, wb_ref, m_ref, *, chunk):
    for r0 in range(0, ya_ref.shape[1], chunk):
        rows = slice(r0, r0 + chunk)
        ua = (ya_ref[0, rows, :].astype(F32) * az_ref[0, rows, :].astype(F32)).astype(BF16)
        ub = (yb_ref[0, rows, :].astype(F32) * bz_ref[0, rows, :].astype(F32)).astype(BF16)
        pa = jnp.dot(ua, wa_ref[...], preferred_element_type=F32)
        pb = jnp.dot(ub, wb_ref[...], preferred_element_type=F32)
        m_ref[0, rows, :] = (ga_ref[0, rows, :].astype(F32) * pa
                             + gb_ref[0, rows, :].astype(F32) * pb).astype(m_ref.dtype)


def _output_kernel(m_ref, x_ref, mod_ref, wo_ref, lng_ref, lnb_ref, o_ref, *, chunk):
    for r0 in range(0, m_ref.shape[1], chunk):
        rows = slice(r0, r0 + chunk)
        out = jnp.dot(m_ref[0, rows, :], wo_ref[...], preferred_element_type=F32)
        z = DEEPNORM_ALPHA * x_ref[0, rows, :] + mod_ref[0, 2:3, :] * out
        mu = jnp.mean(z, axis=-1, keepdims=True)
        zc = z - mu
        var = jnp.mean(zc * zc, axis=-1, keepdims=True)
        o_ref[0, rows, :] = zc * lax.rsqrt(var + LN_EPS) * lng_ref[...] + lnb_ref[...]


def _merge(y_a, y_b, p_silu, p_gate, x, mod, wa, wb, wo, ln_g, ln_b, az_blk, bz_blk, ga_blk, gb_blk,
           tm=512, chunk=256):
    b, s, d = x.shape
    na = y_a.shape[2]
    dv = y_b.shape[2]
    tm = min(tm, s)
    const = dict(pipeline_mode=pl.Buffered(1))
    merged = pl.pallas_call(
        functools.partial(_branch_merge_kernel, chunk=chunk),
        grid=(b, s // tm),
        in_specs=[pl.BlockSpec((1, tm, na), lambda bi, i: (bi, i, 0)),
                  pl.BlockSpec((1, tm, na), lambda bi, i: (bi, i, az_blk)),
                  pl.BlockSpec((1, tm, dv), lambda bi, i: (bi, i, 0)),
                  pl.BlockSpec((1, tm, dv), lambda bi, i: (bi, i, bz_blk)),
                  pl.BlockSpec((1, tm, d), lambda bi, i: (bi, i, ga_blk)),
                  pl.BlockSpec((1, tm, d), lambda bi, i: (bi, i, gb_blk)),
                  pl.BlockSpec((na, d), lambda bi, i: (0, 0), **const),
                  pl.BlockSpec((dv, d), lambda bi, i: (0, 0), **const)],
        out_specs=pl.BlockSpec((1, tm, d), lambda bi, i: (bi, i, 0)),
        out_shape=jax.ShapeDtypeStruct((b, s, d), BF16),
        compiler_params=_params(("parallel", "parallel")),
        name="branch_merge",
    )(y_a, p_silu, y_b, p_silu, p_gate, p_gate, wa, wb)
    return pl.pallas_call(
        functools.partial(_output_kernel, chunk=chunk // 2),
        grid=(b, s // tm),
        in_specs=[pl.BlockSpec((1, tm, d), lambda bi, i: (bi, i, 0)),
                  pl.BlockSpec((1, tm, d), lambda bi, i: (bi, i, 0)),
                  pl.BlockSpec((1, 3, d), lambda bi, i: (bi, 0, 0)),
                  pl.BlockSpec((d, d), lambda bi, i: (0, 0), **const),
                  pl.BlockSpec((1, d), lambda bi, i: (0, 0)),
                  pl.BlockSpec((1, d), lambda bi, i: (0, 0))],
        out_specs=pl.BlockSpec((1, tm, d), lambda bi, i: (bi, i, 0)),
        out_shape=jax.ShapeDtypeStruct((b, s, d), F32),
        compiler_params=_params(("parallel", "parallel")),
        name="output",
    )(merged, x, mod, wo, ln_g, ln_b)


def _rope_tables(s):
    pairs = LANES // 2
    inv_freq = ROPE_BASE ** (-jnp.arange(pairs, dtype=F32) / pairs)

    def table(n):
        ang = jnp.arange(n).astype(F32)[:, None] * inv_freq[None, :]
        return jnp.concatenate([jnp.cos(ang), jnp.cos(ang), -jnp.sin(ang), jnp.sin(ang)], axis=-1)

    return table(s // GRID_W), table(GRID_W)


def _identity_rope_tables(s):
    def table(n):
        return jnp.concatenate([jnp.ones((n, LANES), F32), jnp.zeros((n, LANES), F32)], axis=-1)

    return table(s // GRID_W), table(GRID_W)


def _expand_gate(w_gate2, b_gate, dk):
    dirs, rank, _ = w_gate2.shape
    pairs = dk // 4

    def dup(a):
        a = a.reshape(a.shape[:-1] + (GLA_HEADS, 2, pairs))
        a = jnp.concatenate([a, a], axis=-1)
        return a.reshape(a.shape[:-3] + (GLA_HEADS * dk,))

    w = dup(w_gate2.astype(F32))
    w2e = jnp.zeros((dirs, LANES, GLA_HEADS * dk), F32)
    for d in range(dirs):
        w2e = w2e.at[d, d * rank:(d + 1) * rank].set(w[d])
    b2e = dup(b_gate.astype(F32))[:, None, :]
    return w2e, b2e


def kernel(x, c, ctx, c_ctx, w_mod, b_mod, w_in, na_rpb, gla_w_gate2, gla_b_gate, gla_norm_g,
           w_br_a, w_br_b, w_out, ln_g, ln_b):
    bsz, s, d = x.shape
    lc = ctx.shape[1]
    key_dim, val_dim = d // 2, d
    dk, dv = key_dim // GLA_HEADS, val_dim // GLA_HEADS
    lyr = 0

    widths = (NA_WIDTH,) * 4 + (key_dim, key_dim, val_dim, val_dim, N_DIRS * GATE_RANK, d, d)
    offs = [0]
    for w_ in widths:
        offs.append(offs[-1] + w_)
    o_aq, o_ak, o_av, o_az, o_bq, o_bk, o_bv, o_bz, o_bg, o_ga, o_gb, o_end = offs
    tn = 1024
    tm = min(1024, s)
    w_full = w_in[lyr]
    w_head = w_full.astype(BF16)
    w_sig = w_head[:, o_ga:o_end]
    w_gate1 = jnp.pad(w_head[:, o_bg:o_ga], ((0, 0), (0, LANES - (o_ga - o_bg))))

    def tiles(*ranges):
        return tuple(t for lo, hi in ranges for t in range(lo // tn, hi // tn))

    lin_tiles = tiles((o_aq, o_az), (o_bv, o_bz))
    rot_tiles = tiles((o_bq, o_bv))
    silu_tiles = tiles((o_bz, o_bg), (o_az, o_bq))
    sig_tiles = tiles((0, o_end - o_ga))

    cs = jnp.concatenate([c, c_ctx[None], jnp.zeros((MOD_ROWS - bsz - 1, d), F32)], axis=0)
    mod = _modulation(cs, w_mod[lyr], b_mod[lyr][None])
    mod_lat = mod[:bsz].reshape(bsz, 3, d)
    mod_ctx = jnp.broadcast_to(mod[bsz].reshape(1, 3, d), (bsz, 3, d))

    na_scale = dict(scaled_tiles=NA_WIDTH // tn, scale=NA_HEAD_DIM ** -0.5)
    rot_scale = dict(scaled_tiles=key_dim // tn, scale=dk ** -0.5)
    p_lin, g_lr, h = _ln_projection(x, mod_lat, w_head, lin_tiles, w_gate1, tm, tn, **na_scale)
    tm_act = min(2048, s)
    p_rot = _act_projection(h, w_head, rot_tiles, "rope", tm_act, tn, tables=_rope_tables(s), **rot_scale)
    p_silu = _act_projection(h, w_head, silu_tiles, "silu", tm_act, tn)
    p_sig = _act_projection(h, w_sig, sig_tiles, "sigmoid", tm_act, tn)
    p_lin_c, g_lr_c, h_c = _ln_projection(ctx, mod_ctx, w_head, lin_tiles, w_gate1, lc, tn, **na_scale)
    p_rot_c = _act_projection(h_c, w_head, rot_tiles, "rope", lc, tn, tables=_identity_rope_tables(lc),
                              **rot_scale)

    y_a = _neighbourhood_attention(p_lin, p_lin_c, _na_bias_table(na_rpb[lyr]))

    w2e, b2e = _expand_gate(gla_w_gate2[lyr], gla_b_gate[lyr], dk)
    w2e = w2e.astype(BF16)
    blocks = dict(q_blk=0, k_blk=key_dim // dk, v_blk=3 * NA_WIDTH // dv, dk=dk, dv=dv)
    s_f = _gla_scan(p_rot_c, p_lin_c, g_lr_c, w2e, b2e, direction=0, tb=lc, emit_o=False, emit_state=True,
                    **blocks)
    s_b = _gla_scan(p_rot_c, p_lin_c, g_lr_c, w2e, b2e, direction=1, tb=lc, emit_o=False, emit_state=True,
                    **blocks)
    tb = min(2048, s)
    o_b =_gla_scan(p_rot, p_lin, g_lr, w2e, b2e, direction=1, tb=tb, s0=s_b, **blocks)
    y_b = _gla_scan(p_rot, p_lin, g_lr, w2e, b2e, direction=0, tb=tb, s0=s_f, ob=o_b,
                    norm_g=gla_norm_g[lyr][None].astype(F32), **blocks)

    return _merge(y_a, y_b, p_silu, p_sig, x, mod_lat,
                  w_br_a[lyr].astype(BF16), w_br_b[lyr].astype(BF16), w_out[lyr].astype(BF16),
                  ln_g[lyr][None], ln_b[lyr][None],
                  az_blk=val_dim // NA_WIDTH, bz_blk=0, ga_blk=0, gb_blk=1)
```

```python
import functools

import jax
import jax.numpy as jnp
from jax import lax
from jax.experimental import pallas as pl
from jax.experimental.pallas import tpu as pltpu

F32 = jnp.float32
BF16 = jnp.bfloat16

GRID_W = 64
WIN_H = 8
WIN_W = 16
NA_HEADS = 16
NA_HEAD_DIM = 64
NA_WIDTH = NA_HEADS * NA_HEAD_DIM
GLA_HEADS = 4
GATE_RANK = 16
GATE_NORMALIZER = 16.0
GLA_CHUNK = 64
GLA_BLOCK = 4 * GLA_CHUNK
N_DIRS = 2
ROPE_BASE = 10000.0
DEPTH = 1
DEEPNORM_ALPHA = (2 * DEPTH) ** 0.25
LN_EPS = 1e-6
RMS_EPS = 1e-6
MASK_VALUE = -1e30

LANES = 128
MOD_ROWS = 8
VMEM_LIMIT = 56 * 2 ** 20


def _params(sem, vmem=VMEM_LIMIT):
    return pltpu.CompilerParams(dimension_semantics=sem, vmem_limit_bytes=vmem)


def _mod_kernel(c_ref, w_ref, b_ref, o_ref):
    c = c_ref[...]
    s = c * jax.nn.sigmoid(c)
    o_ref[...] = jnp.dot(s, w_ref[...], preferred_element_type=F32) + b_ref[...]


def _modulation(cs, w, b, tn=768):
    rows, d = cs.shape
    n = w.shape[1]
    return pl.pallas_call(
        _mod_kernel,
        grid=(n // tn,),
        in_specs=[pl.BlockSpec((rows, d), lambda j: (0, 0)),
                  pl.BlockSpec((d, tn), lambda j: (0, j)),
                  pl.BlockSpec((1, tn), lambda j: (0, j))],
        out_specs=pl.BlockSpec((rows, tn), lambda j: (0, j)),
        out_shape=jax.ShapeDtypeStruct((rows, n), F32),
        compiler_params=_params(("parallel",)),
        name="mod",
    )(cs, w, b)


def _ln_proj_kernel(x_ref, mod_ref, w_ref, wg_ref, p_ref, g_ref, h_ref, *, ln_rows, scaled_tiles, scale):
    n = pl.program_id(2)
    tm = x_ref.shape[1]
    col_scale = jnp.where(n < scaled_tiles, scale, 1.0)

    @pl.when(n == 0)
    def _():
        shift = mod_ref[0, 0:1, :]
        scale1 = 1.0 + mod_ref[0, 1:2, :]
        for r0 in range(0, tm, ln_rows):
            rows = slice(r0, r0 + ln_rows)
            xs = x_ref[0, rows, :]
            mu = jnp.mean(xs, axis=-1, keepdims=True)
            xc = xs - mu
            var = jnp.mean(xc * xc, axis=-1, keepdims=True)
            hb = (xc * lax.rsqrt(var + LN_EPS) * scale1 + shift).astype(BF16)
            h_ref[0, rows, :] = hb
            g_ref[0, rows, :] = jnp.dot(hb, wg_ref[...], preferred_element_type=F32)
            p_ref[0, rows, :] = (jnp.dot(hb, w_ref[...], preferred_element_type=F32) * col_scale).astype(p_ref.dtype)

    @pl.when(n > 0)
    def _():
        acc = jnp.dot(h_ref[0], w_ref[...], preferred_element_type=F32)
        p_ref[0] = (acc * col_scale).astype(p_ref.dtype)


def _column_tile_map(col_tiles):
    def index_map(bi, i, j):
        tile = col_tiles[0]
        for pos in range(1, len(col_tiles)):
            tile = jnp.where(j >= pos, col_tiles[pos], tile)
        return (0, tile)
    return index_map


def _ln_projection(xin, mod, w, col_tiles, wg, tm, tn, scaled_tiles, scale, ln_rows=256):
    b, t, d = xin.shape
    n = len(col_tiles) * tn
    kern = functools.partial(_ln_proj_kernel, ln_rows=min(ln_rows, tm), scaled_tiles=scaled_tiles, scale=scale)
    return pl.pallas_call(
        kern,
        grid=(b, t // tm, n // tn),
        in_specs=[pl.BlockSpec((1, tm, d), lambda bi, i, j: (bi, i, 0)),
                  pl.BlockSpec((1, 3, d), lambda bi, i, j: (bi, 0, 0)),
                  pl.BlockSpec((d, tn), _column_tile_map(col_tiles)),
                  pl.BlockSpec((d, LANES), lambda bi, i, j: (0, 0))],
        out_specs=[pl.BlockSpec((1, tm, tn), lambda bi, i, j: (bi, i, j)),
                   pl.BlockSpec((1, tm, LANES), lambda bi, i, j: (bi, i, 0)),
                   pl.BlockSpec((1, tm, d), lambda bi, i, j: (bi, i, 0))],
        out_shape=[jax.ShapeDtypeStruct((b, t, n), BF16),
                   jax.ShapeDtypeStruct((b, t, LANES), F32),
                   jax.ShapeDtypeStruct((b, t, d), BF16)],
        compiler_params=_params(("parallel", "parallel", "arbitrary")),
        name="ln_proj",
    )(xin, mod, w, wg)


def _act_proj_kernel(*refs, kind, scaled_tiles, scale):
    if kind == "rope":
        h_ref, w_ref, rowtab_ref, coltab_ref, o_ref = refs
    else:
        h_ref, w_ref, o_ref = refs
    acc = jnp.dot(h_ref[0], w_ref[...], preferred_element_type=F32)
    if kind == "silu":
        half = 0.5 * acc
        o_ref[0] = (half * jnp.tanh(half) + half).astype(o_ref.dtype)
    elif kind == "sigmoid":
        o_ref[0] = (0.5 * jnp.tanh(0.5 * acc) + 0.5).astype(o_ref.dtype)
    else:
        acc = acc * jnp.where(pl.program_id(2) < scaled_tiles, scale, 1.0)
        for r in range(acc.shape[0] // GRID_W):
            rows = slice(r * GRID_W, (r + 1) * GRID_W)
            for j in range(acc.shape[1] // LANES):
                lanes = slice(j * LANES, (j + 1) * LANES)
                u = acc[rows, lanes]
                if j % 2 == 0:
                    c, s = rowtab_ref[r:r + 1, :LANES], rowtab_ref[r:r + 1, LANES:]
                else:
                    c, s = coltab_ref[:, :LANES], coltab_ref[:, LANES:]
                o_ref[0, rows, lanes] = (u * c + pltpu.roll(u, LANES // 2, 1) * s).astype(o_ref.dtype)


def _act_projection(h, w, col_tiles, kind, tm, tn, tables=(), scaled_tiles=0, scale=1.0):
    b, t, d = h.shape
    n = len(col_tiles) * tn
    kern = functools.partial(_act_proj_kernel, kind=kind, scaled_tiles=scaled_tiles, scale=scale)
    in_specs = [pl.BlockSpec((1, tm, d), lambda bi, i, j: (bi, i, 0)),
                pl.BlockSpec((d, tn), _column_tile_map(col_tiles))]
    if tables:
        in_specs += [pl.BlockSpec((tm // GRID_W, 2 * LANES), lambda bi, i, j: (i, 0)),
                     pl.BlockSpec((GRID_W, 2 * LANES), lambda bi, i, j: (0, 0))]
    return pl.pallas_call(
        kern,
        grid=(b, t // tm, n // tn),
        in_specs=in_specs,
        out_specs=pl.BlockSpec((1, tm, tn), lambda bi, i, j: (bi, i, j)),
        out_shape=jax.ShapeDtypeStruct((b, t, n), BF16),
        compiler_params=_params(("parallel", "parallel", "arbitrary")),
        name="proj_" + kind,
    )(h, w, *tables)


def _na_halo_rows(rb, rows):
    return min(rb + WIN_H, rows)


def _na_halo_start(blk, rb, rows):
    return jnp.clip(blk * rb - WIN_H // 2, 0, rows - _na_halo_rows(rb, rows))


def _na_kernel(q_ref, k_ref, v_ref, kc_ref, vc_ref, bias_ref, o_ref, s_buf, p_buf, l_buf, *, rb, rows):
    blk = pl.program_id(2)
    hd = NA_HEAD_DIM
    n_loc = WIN_H * GRID_W
    is_a = lax.broadcasted_iota(jnp.int32, (GRID_W, 2 * hd), 1) < hd
    contract_last = (((1,), (1,)), ((), ()))

    halo_start = _na_halo_start(blk, rb, rows)

    def window_start(i):
        r = blk * rb + i
        r_start = jnp.clip(r - WIN_H // 2, 0, rows - WIN_H)
        return r_start - r + (WIN_H - 1), pl.multiple_of((r_start - halo_start) * GRID_W, GRID_W)

    def scores(i):
        cls, k0 = window_start(i)
        q = q_ref[0, i * GRID_W:(i + 1) * GRID_W, :]
        zero = jnp.zeros_like(q)
        qs = jnp.concatenate([jnp.where(is_a, q, zero), jnp.where(is_a, zero, q)], axis=0)
        k8 = k_ref[0, pl.ds(k0, n_loc), :]
        bias = jnp.concatenate([bias_ref[0, cls + 2 * jj] for jj in range(WIN_H // 2)], axis=1)
        s_buf[i % 2, :, :n_loc] = lax.dot_general(qs, k8, contract_last, preferred_element_type=F32) + bias
        s_buf[i % 2, :, n_loc:] = lax.dot_general(qs, kc_ref[0], contract_last, preferred_element_type=F32)

    def softmax(i):
        s = s_buf[i % 2]
        m = jnp.max(s, axis=-1, keepdims=True)
        p = jnp.exp(s - m)
        l_buf[i % 2] = jnp.sum(p, axis=-1, keepdims=True)
        p_buf[i % 2] = p.astype(BF16)

    def values(i):
        _, k0 = window_start(i)
        v8 = v_ref[0, pl.ds(k0, n_loc), :]
        o = (jnp.dot(p_buf[i % 2, :, :n_loc], v8, preferred_element_type=F32)
             + jnp.dot(p_buf[i % 2, :, n_loc:], vc_ref[0], preferred_element_type=F32))
        o = o / l_buf[i % 2]
        out = jnp.where(is_a, o[:GRID_W], o[GRID_W:])
        o_ref[0, i * GRID_W:(i + 1) * GRID_W, :] = out.astype(o_ref.dtype)

    for t in range(rb + 2):
        if t >= 2:
            values(t - 2)
        if 1 <= t <= rb:
            softmax(t - 1)
        if t < rb:
            scores(t)


def _neighbourhood_attention(p, p_c, bias, rb=64):
    b, s, _ = p.shape
    lc = p_c.shape[1]
    rows = s // GRID_W
    rb = min(rb, rows)
    hg = NA_WIDTH // LANES
    kern = functools.partial(_na_kernel, rb=rb, rows=rows)
    n_keys = WIN_H * GRID_W + lc
    halo_tokens = _na_halo_rows(rb, rows) * GRID_W

    def halo_spec(first_group):
        return pl.BlockSpec(
            (pl.Element(1), pl.Element(halo_tokens), pl.Element(LANES)),
            lambda bi, g, i: (bi, _na_halo_start(i, rb, rows) * GRID_W, (first_group + g) * LANES))

    return pl.pallas_call(
        kern,
        grid=(b, hg, rows // rb),
        in_specs=[pl.BlockSpec((1, rb * GRID_W, LANES), lambda bi, g, i: (bi, i, g)),
                  halo_spec(hg), halo_spec(2 * hg),
                  pl.BlockSpec((1, lc, LANES), lambda bi, g, i: (bi, 0, hg + g)),
                  pl.BlockSpec((1, lc, LANES), lambda bi, g, i: (bi, 0, 2 * hg + g)),
                  pl.BlockSpec((1, 2 * WIN_H - 2, 2 * GRID_W, 2 * GRID_W), lambda bi, g, i: (g, 0, 0, 0))],
        out_specs=pl.BlockSpec((1, rb * GRID_W, LANES), lambda bi, g, i: (bi, i, g)),
        out_shape=jax.ShapeDtypeStruct((b, s, NA_WIDTH), BF16),
        scratch_shapes=[pltpu.VMEM((2, 2 * GRID_W, n_keys), F32),
                        pltpu.VMEM((2, 2 * GRID_W, n_keys), BF16),
                        pltpu.VMEM((2, 2 * GRID_W, 1), F32)],
        compiler_params=_params(("parallel", "parallel", "arbitrary")),
        name="na",
    )(p, p, p, p_c, p_c, bias)


def _na_bias_table(rpb):
    qcol = jnp.arange(GRID_W)[:, None]
    kcol = jnp.arange(GRID_W)[None, :]
    c_start = jnp.clip(qcol - WIN_W // 2, 0, GRID_W - WIN_W)
    in_win = (kcol >= c_start) & (kcol < c_start + WIN_W)
    ring = 2 * GRID_W
    fill = jnp.full(rpb.shape[:2] + (ring - (2 * WIN_W - 1),), MASK_VALUE, rpb.dtype)
    ext = jnp.concatenate([rpb[:, :, WIN_W - 1:], fill, rpb[:, :, :WIN_W - 1]], axis=-1)
    toep = jnp.tile(ext, (1, 1, GRID_W))[:, :, :GRID_W * (ring - 1)]
    toep = toep.reshape(rpb.shape[:2] + (GRID_W, ring - 1))[..., :GRID_W]
    band = jnp.where(in_win[None, None], toep, MASK_VALUE).astype(F32)
    n_dr = band.shape[1]
    band = band.reshape(NA_HEADS // 2, 2, n_dr, GRID_W, GRID_W).transpose(0, 2, 1, 3, 4)
    band = band.reshape(NA_HEADS // 2, n_dr, 2 * GRID_W, GRID_W)
    return jnp.concatenate([band[:, :-1], band[:, 1:]], axis=-1)


def _gla_kernel(*refs, reverse, has_init, emit_o, combine, emit_state):
    refs = list(refs)
    q_ref, k_ref, v_ref, lr_ref, w2_ref, b2_ref = refs[:6]
    pos = 6
    s0_ref = ob_ref = ng_ref = out_ref = sfin_ref = None
    if has_init:
        s0_ref = refs[pos]; pos += 1
    if combine:
        ob_ref, ng_ref = refs[pos], refs[pos + 1]; pos += 2
    if emit_o:
        out_ref = refs[pos]; pos += 1
    if emit_state:
        sfin_ref = refs[pos]; pos += 1
    s_ref, cum_buf, kend_buf, dec_buf, att_buf, qin_buf = refs[pos:pos + 6]

    j = pl.program_id(2)
    nblk = pl.num_programs(2)
    blk_len = GLA_BLOCK
    half = blk_len // 2
    c_len = GLA_CHUNK

    @pl.when(j == 0)
    def _():
        if has_init:
            s_ref[...] = s0_ref[0, 0]
        else:
            s_ref[...] = jnp.zeros_like(s_ref)

    ri = lax.broadcasted_iota(jnp.int32, (blk_len, blk_len), 0)
    ci = lax.broadcasted_iota(jnp.int32, (blk_len, blk_len), 1)
    tri = (ci >= ri) if reverse else (ci <= ri)
    tri_b = tri.astype(BF16)
    tri_h = tri[:half, :half]
    contract_last = (((1,), (1,)), ((), ()))
    contract_first = (((0,), (0,)), ((), ()))

    def bcast_halves(lo_row, hi_row):
        return jnp.concatenate([jnp.broadcast_to(lo_row, (half, lo_row.shape[1])),
                                jnp.broadcast_to(hi_row, (half, hi_row.shape[1]))], axis=0)

    def decay(r0):
        rows = slice(r0, r0 + blk_len)
        z = jnp.dot(lr_ref[0, rows, :].astype(BF16), w2_ref[0], preferred_element_type=F32) + b2_ref[0]
        g = (jnp.minimum(z, 0.0) - jnp.log(1.0 + jnp.exp(-jnp.abs(z)))) * (1.0 / GATE_NORMALIZER)
        g_hi = g.astype(BF16)
        g_lo = (g - g_hi.astype(F32)).astype(BF16)
        cum_buf[rows, :] = (jnp.dot(tri_b, g_hi, preferred_element_type=F32)
                            + jnp.dot(tri_b, g_lo, preferred_element_type=F32))

    def prepare(r0, slot):
        rows = slice(r0, r0 + blk_len)
        cum = cum_buf[rows, :]

        def row(i):
            return cum_buf[r0 + i:r0 + i + 1, :]

        if reverse:
            m_lo, m_hi, mid, end = row(c_len), row(half + c_len), row(half), row(0)
        else:
            m_lo, m_hi, mid, end = row(c_len - 1), row(half + c_len - 1), row(half - 1), row(blk_len - 1)
        m_ref = bcast_halves(m_lo, m_hi)
        def expb(a):
            return jnp.exp(a).astype(BF16)

        km = k_ref[0, rows, :] * expb(m_ref - cum)
        kend_buf[slot] = km * bcast_halves(expb(end - m_lo), expb(end - m_hi))
        dec_buf[slot] = jnp.exp(end)
        if emit_o:
            qm = q_ref[0, rows, :] * expb(cum - m_ref)
            a_lo = lax.dot_general(qm[:half], km[:half], contract_last, preferred_element_type=F32)
            a_hi = lax.dot_general(qm[half:], km[half:], contract_last, preferred_element_type=F32)
            a_lo = jnp.where(tri_h, a_lo, 0.0)
            a_hi = jnp.where(tri_h, a_hi, 0.0)
            zeros = jnp.zeros((half, half), F32)
            if reverse:
                a_off = lax.dot_general(qm[:half] * expb(m_lo - mid), km[half:] * expb(mid - m_hi),
                                        contract_last, preferred_element_type=F32)
                att = jnp.concatenate([jnp.concatenate([a_lo, a_off], axis=1),
                                       jnp.concatenate([zeros, a_hi], axis=1)], axis=0)
            else:
                a_off = lax.dot_general(qm[half:] * expb(m_hi - mid), km[:half] * expb(mid - m_lo),
                                        contract_last, preferred_element_type=F32)
                att = jnp.concatenate([jnp.concatenate([a_lo, zeros], axis=1),
                                       jnp.concatenate([a_off, a_hi], axis=1)], axis=0)
            att_buf[slot] = att.astype(BF16)
            qin_buf[slot] = qm * bcast_halves(expb(m_lo), expb(m_hi))

    def advance(r0, slot):
        rows = slice(r0, r0 + blk_len)
        v = v_ref[0, rows, :]
        st = s_ref[...]
        if emit_o:
            o = (jnp.dot(att_buf[slot], v, preferred_element_type=F32)
                 + lax.dot_general(qin_buf[slot], st.astype(BF16), contract_last, preferred_element_type=F32))
            if combine:
                tot = o + ob_ref[0, rows, :].astype(F32)
                ms = jnp.mean(tot * tot, axis=-1, keepdims=True)
                o = tot * lax.rsqrt(ms + RMS_EPS) * ng_ref[...]
            out_ref[0, rows, :] = o.astype(out_ref.dtype)
        s_ref[...] = st * dec_buf[slot] + lax.dot_general(v, kend_buf[slot], contract_first,
                                                          preferred_element_type=F32)

    nb = q_ref.shape[1] // blk_len
    order = list(range(nb - 1, -1, -1) if reverse else range(nb))
    for bi in order:
        decay(bi * blk_len)
    prepare(order[0] * blk_len, 0)
    for n, bi in enumerate(order):
        if n + 1 < nb:
            prepare(order[n + 1] * blk_len, (n + 1) % 2)
        advance(bi * blk_len, n % 2)

    if emit_state:
        @pl.when(j == nblk - 1)
        def _():
            sfin_ref[0, 0] = s_ref[...]


def _gla_scan(p_qk, p_v, g_lr, w2e, b2e, *, direction, tb, q_blk, k_blk, v_blk, dk, dv,
              s0=None, ob=None, norm_g=None, emit_o=True, emit_state=False):
    b, t, _ = p_qk.shape
    nblk = t // tb
    reverse = direction == 1
    combine = ob is not None
    has_init = s0 is not None

    def blk(j):
        return (nblk - 1 - j) if reverse else j

    in_specs = [pl.BlockSpec((1, tb, dk), lambda bi, h, j: (bi, blk(j), q_blk + h)),
                pl.BlockSpec((1, tb, dk), lambda bi, h, j: (bi, blk(j), k_blk + h)),
                pl.BlockSpec((1, tb, dv), lambda bi, h, j: (bi, blk(j), v_blk + h)),
                pl.BlockSpec((1, tb, LANES), lambda bi, h, j: (bi, blk(j), 0)),
                pl.BlockSpec((1, LANES, dk), lambda bi, h, j: (direction, 0, h)),
                pl.BlockSpec((1, 1, dk), lambda bi, h, j: (direction, 0, h))]
    args = [p_qk, p_qk, p_v, g_lr, w2e, b2e]
    if has_init:
        in_specs.append(pl.BlockSpec((1, 1, dv, dk), lambda bi, h, j: (bi, h, 0, 0)))
        args.append(s0)
    if combine:
        in_specs.append(pl.BlockSpec((1, tb, dv), lambda bi, h, j: (bi, blk(j), h)))
        in_specs.append(pl.BlockSpec((1, dv), lambda bi, h, j: (0, 0)))
        args += [ob, norm_g]
    out_specs, out_shape = [], []
    if emit_o:
        out_specs.append(pl.BlockSpec((1, tb, dv), lambda bi, h, j: (bi, blk(j), h)))
        out_shape.append(jax.ShapeDtypeStruct((b, t, GLA_HEADS * dv), BF16))
    if emit_state:
        out_specs.append(pl.BlockSpec((1, 1, dv, dk), lambda bi, h, j: (bi, h, 0, 0)))
        out_shape.append(jax.ShapeDtypeStruct((b, GLA_HEADS, dv, dk), F32))
    assert tb % GLA_BLOCK == 0
    kern = functools.partial(_gla_kernel, reverse=reverse, has_init=has_init,
                             emit_o=emit_o, combine=combine, emit_state=emit_state)
    res = pl.pallas_call(
        kern,
        grid=(b, GLA_HEADS, nblk),
        in_specs=in_specs,
        out_specs=out_specs,
        out_shape=out_shape,
        scratch_shapes=[pltpu.VMEM((dv, dk), F32),
                        pltpu.VMEM((tb, dk), F32),
                        pltpu.VMEM((2, GLA_BLOCK, dk), BF16),
                        pltpu.VMEM((2, 1, dk), F32),
                        pltpu.VMEM((2, GLA_BLOCK, GLA_BLOCK), BF16),
                        pltpu.VMEM((2, GLA_BLOCK, dk), BF16)],
        compiler_params=_params(("parallel", "parallel", "arbitrary")),
        name="gla_" + ("bwd" if reverse else "fwd") + ("_ctx" if not emit_o else ""),
    )(*args)
    return res[0] if len(res) == 1 else res


def _branch_merge_kernel(ya_ref, az_ref, yb_ref, bz_ref, ga_ref, gb_ref, wa_ref, wb_ref, m_ref, *, chunk):
    for r0 in range(0, ya_ref.shape[1], chunk):
        rows = slice(r0, r0 + chunk)
        ua = (ya_ref[0, rows, :].astype(F32) * az_ref[0, rows, :].astype(F32)).astype(BF16)
        ub = (yb_ref[0, rows, :].astype(F32) * bz_ref[0, rows, :].astype(F32)).astype(BF16)
        pa = jnp.dot(ua, wa_ref[...], preferred_element_type=F32)
        pb = jnp.dot(ub, wb_ref[...], preferred_element_type=F32)
        m_ref[0, rows, :] = (ga_ref[0, rows, :].astype(F32) * pa
                             + gb_ref[0, rows, :].astype(F32) * pb).astype(m_ref.dtype)


def _output_kernel(m_ref, x_ref, mod_ref, wo_ref, lng_ref, lnb_ref, o_ref, *, chunk):
    for r0 in range(0, m_ref.shape[1], chunk):
        rows = slice(r0, r0 + chunk)
        out = jnp.dot(m_ref[0, rows, :], wo_ref[...], preferred_element_type=F32)
        z = DEEPNORM_ALPHA * x_ref[0, rows, :] + mod_ref[0, 2:3, :] * out
        mu = jnp.mean(z, axis=-1, keepdims=True)
        zc = z - mu
        var = jnp.mean(zc * zc, axis=-1, keepdims=True)
        o_ref[0, rows, :] = zc * lax.rsqrt(var + LN_EPS) * lng_ref[...] + lnb_ref[...]


def _merge(y_a, y_b, p_silu, p_gate, x, mod, wa, wb, wo, ln_g, ln_b, az_blk, bz_blk, ga_blk, gb_blk,
           tm=512, chunk=256):
    b, s, d = x.shape
    na = y_a.shape[2]
    dv = y_b.shape[2]
    tm = min(tm, s)
    const = dict(pipeline_mode=pl.Buffered(1))
    merged = pl.pallas_call(
        functools.partial(_branch_merge_kernel, chunk=chunk),
        grid=(b, s // tm),
        in_specs=[pl.BlockSpec((1, tm, na), lambda bi, i: (bi, i, 0)),
                  pl.BlockSpec((1, tm, na), lambda bi, i: (bi, i, az_blk)),
                  pl.BlockSpec((1, tm, dv), lambda bi, i: (bi, i, 0)),
                  pl.BlockSpec((1, tm, dv), lambda bi, i: (bi, i, bz_blk)),
                  pl.BlockSpec((1, tm, d), lambda bi, i: (bi, i, ga_blk)),
                  pl.BlockSpec((1, tm, d), lambda bi, i: (bi, i, gb_blk)),
                  pl.BlockSpec((na, d), lambda bi, i: (0, 0), **const),
                  pl.BlockSpec((dv, d), lambda bi, i: (0, 0), **const)],
        out_specs=pl.BlockSpec((1, tm, d), lambda bi, i: (bi, i, 0)),
        out_shape=jax.ShapeDtypeStruct((b, s, d), BF16),
        compiler_params=_params(("parallel", "parallel")),
        name="branch_merge",
    )(y_a, p_silu, y_b, p_silu, p_gate, p_gate, wa, wb)
    return pl.pallas_call(
        functools.partial(_output_kernel, chunk=chunk // 2),
        grid=(b, s // tm),
        in_specs=[pl.BlockSpec((1, tm, d), lambda bi, i: (bi, i, 0)),
                  pl.BlockSpec((1, tm, d), lambda bi, i: (bi, i, 0)),
                  pl.BlockSpec((1, 3, d), lambda bi, i: (bi, 0, 0)),
                  pl.BlockSpec((d, d), lambda bi, i: (0, 0), **const),
                  pl.BlockSpec((1, d), lambda bi, i: (0, 0)),
                  pl.BlockSpec((1, d), lambda bi, i: (0, 0))],
        out_specs=pl.BlockSpec((1, tm, d), lambda bi, i: (bi, i, 0)),
        out_shape=jax.ShapeDtypeStruct((b, s, d), F32),
        compiler_params=_params(("parallel", "parallel")),
        name="output",
    )(merged, x, mod, wo, ln_g, ln_b)


def _rope_tables(s):
    pairs = LANES // 2
    inv_freq = ROPE_BASE ** (-jnp.arange(pairs, dtype=F32) / pairs)

    def table(n):
        ang = jnp.arange(n).astype(F32)[:, None] * inv_freq[None, :]
        return jnp.concatenate([jnp.cos(ang), jnp.cos(ang), -jnp.sin(ang), jnp.sin(ang)], axis=-1)

    return table(s // GRID_W), table(GRID_W)


def _identity_rope_tables(s):
    def table(n):
        return jnp.concatenate([jnp.ones((n, LANES), F32), jnp.zeros((n, LANES), F32)], axis=-1)

    return table(s // GRID_W), table(GRID_W)


def _expand_gate(w_gate2, b_gate, dk):
    dirs, rank, _ = w_gate2.shape
    pairs = dk // 4

    def dup(a):
        a = a.reshape(a.shape[:-1] + (GLA_HEADS, 2, pairs))
        a = jnp.concatenate([a, a], axis=-1)
        return a.reshape(a.shape[:-3] + (GLA_HEADS * dk,))

    w = dup(w_gate2.astype(F32))
    w2e = jnp.zeros((dirs, LANES, GLA_HEADS * dk), F32)
    for d in range(dirs):
        w2e = w2e.at[d, d * rank:(d + 1) * rank].set(w[d])
    b2e = dup(b_gate.astype(F32))[:, None, :]
    return w2e, b2e


def kernel(x, c, ctx, c_ctx, w_mod, b_mod, w_in, na_rpb, gla_w_gate2, gla_b_gate, gla_norm_g,
           w_br_a, w_br_b, w_out, ln_g, ln_b):
    bsz, s, d = x.shape
    lc = ctx.shape[1]
    key_dim, val_dim = d // 2, d
    dk, dv = key_dim // GLA_HEADS, val_dim // GLA_HEADS
    lyr = 0

    widths = (NA_WIDTH,) * 4 + (key_dim, key_dim, val_dim, val_dim, N_DIRS * GATE_RANK, d, d)
    offs = [0]
    for w_ in widths:
        offs.append(offs[-1] + w_)
    o_aq, o_ak, o_av, o_az, o_bq, o_bk, o_bv, o_bz, o_bg, o_ga, o_gb, o_end = offs
    tn = 1024
    tm = min(1024, s)
    w_full = w_in[lyr]
    w_head = w_full.astype(BF16)
    w_sig = w_head[:, o_ga:o_end]
    w_gate1 = jnp.pad(w_head[:, o_bg:o_ga], ((0, 0), (0, LANES - (o_ga - o_bg))))

    def tiles(*ranges):
        return tuple(t for lo, hi in ranges for t in range(lo // tn, hi // tn))

    lin_tiles = tiles((o_aq, o_az), (o_bv, o_bz))
    rot_tiles = tiles((o_bq, o_bv))
    silu_tiles = tiles((o_bz, o_bg), (o_az, o_bq))
    sig_tiles = tiles((0, o_end - o_ga))

    cs = jnp.concatenate([c, c_ctx[None], jnp.zeros((MOD_ROWS - bsz - 1, d), F32)], axis=0)
    mod = _modulation(cs, w_mod[lyr], b_mod[lyr][None])
    mod_lat = mod[:bsz].reshape(bsz, 3, d)
    mod_ctx = mod[bsz].reshape(1, 3, d)

    na_scale = dict(scaled_tiles=NA_WIDTH // tn, scale=NA_HEAD_DIM ** -0.5)
    rot_scale = dict(scaled_tiles=key_dim // tn, scale=dk ** -0.5)
    p_lin, g_lr, h = _ln_projection(x, mod_lat, w_head, lin_tiles, w_gate1, tm, tn, **na_scale)
    tm_act = min(2048, s)
    p_rot = _act_projection(h, w_head, rot_tiles, "rope", tm_act, tn, tables=_rope_tables(s), **rot_scale)
    p_silu = _act_projection(h, w_head, silu_tiles, "silu", tm_act, tn)
    p_sig = _act_projection(h, w_sig, sig_tiles, "sigmoid", tm_act, tn)
    ctx_rows = ctx.reshape(1, bsz * lc, d)
    p_lin_c, g_lr_c, h_c = _ln_projection(ctx_rows, mod_ctx, w_head, lin_tiles, w_gate1, bsz * lc, tn, **na_scale)
    p_rot_c = _act_projection(h_c, w_head, rot_tiles, "rope", bsz * lc, tn,
                              tables=_identity_rope_tables(bsz * lc), **rot_scale)
    p_lin_c, g_lr_c, p_rot_c = (a.reshape(bsz, lc, a.shape[-1]) for a in (p_lin_c, g_lr_c, p_rot_c))

    y_a = _neighbourhood_attention(p_lin, p_lin_c, _na_bias_table(na_rpb[lyr]))

    w2e, b2e = _expand_gate(gla_w_gate2[lyr], gla_b_gate[lyr], dk)
    w2e = w2e.astype(BF16)
    blocks = dict(q_blk=0, k_blk=key_dim // dk, v_blk=3 * NA_WIDTH // dv, dk=dk, dv=dv)
    s_f = _gla_scan(p_rot_c, p_lin_c, g_lr_c, w2e, b2e, direction=0, tb=lc, emit_o=False, emit_state=True,
                    **blocks)
    s_b = _gla_scan(p_rot_c, p_lin_c, g_lr_c, w2e, b2e, direction=1, tb=lc, emit_o=False, emit_state=True,
                    **blocks)
    tb = min(4096, s)
    o_b =_gla_scan(p_rot, p_lin, g_lr, w2e, b2e, direction=1, tb=tb, s0=s_b, **blocks)
    y_b = _gla_scan(p_rot, p_lin, g_lr, w2e, b2e, direction=0, tb=tb, s0=s_f, ob=o_b,
                    norm_g=gla_norm_g[lyr][None].astype(F32), **blocks)

    return _merge(y_a, y_b, p_silu, p_sig, x, mod_lat,
                  w_br_a[lyr].astype(BF16), w_br_b[lyr].astype(BF16), w_out[lyr].astype(BF16),
                  ln_g[lyr][None], ln_b[lyr][None],
                  az_blk=val_dim // NA_WIDTH, bz_blk=0, ga_blk=0, gb_blk=1)
```

```python
import functools

import jax
import jax.numpy as jnp
from jax import lax
from jax.experimental import pallas as pl
from jax.experimental.pallas import tpu as pltpu

F32 = jnp.float32
BF16 = jnp.bfloat16

GRID_W = 64
WIN_H = 8
WIN_W = 16
NA_HEADS = 16
NA_HEAD_DIM = 64
NA_WIDTH = NA_HEADS * NA_HEAD_DIM
GLA_HEADS = 4
GATE_RANK = 16
GATE_NORMALIZER = 16.0
GLA_CHUNK = 64
GLA_BLOCK = 4 * GLA_CHUNK
N_DIRS = 2
ROPE_BASE = 10000.0
DEPTH = 1
DEEPNORM_ALPHA = (2 * DEPTH) ** 0.25
LN_EPS = 1e-6
RMS_EPS = 1e-6
MASK_VALUE = -1e30

LANES = 128
MOD_ROWS = 8
VMEM_LIMIT = 56 * 2 ** 20

MOD_COL_TILE = 768
PROJ_COL_TILE = 1024
LN_ROW_TILE = 1024
LN_ROW_CHUNK = 256
ACT_ROW_TILE = 2048
NA_ROWS_PER_STEP = 64
GLA_ROWS_PER_STEP = 4096
MERGE_ROW_TILE = 512
MERGE_ROW_CHUNK = 256


def _params(sem, vmem=VMEM_LIMIT):
    return pltpu.CompilerParams(dimension_semantics=sem, vmem_limit_bytes=vmem)


def _mod_kernel(c_ref, w_ref, b_ref, o_ref):
    c = c_ref[...]
    s = c * jax.nn.sigmoid(c)
    o_ref[...] = jnp.dot(s, w_ref[...], preferred_element_type=F32) + b_ref[...]


def _modulation(cs, w, b, tn=MOD_COL_TILE):
    rows, d = cs.shape
    n = w.shape[1]
    assert n % tn == 0
    return pl.pallas_call(
        _mod_kernel,
        grid=(n // tn,),
        in_specs=[pl.BlockSpec((rows, d), lambda j: (0, 0)),
                  pl.BlockSpec((d, tn), lambda j: (0, j)),
                  pl.BlockSpec((1, tn), lambda j: (0, j))],
        out_specs=pl.BlockSpec((rows, tn), lambda j: (0, j)),
        out_shape=jax.ShapeDtypeStruct((rows, n), F32),
        compiler_params=_params(("parallel",)),
        name="mod",
    )(cs, w, b)


def _ln_proj_kernel(x_ref, mod_ref, w_ref, wg_ref, p_ref, g_ref, h_ref, *, ln_rows, scaled_tiles, scale):
    n = pl.program_id(2)
    tm = x_ref.shape[1]
    col_scale = jnp.where(n < scaled_tiles, scale, 1.0)

    @pl.when(n == 0)
    def _():
        shift = mod_ref[0, 0:1, :]
        scale1 = 1.0 + mod_ref[0, 1:2, :]
        for r0 in range(0, tm, ln_rows):
            rows = slice(r0, r0 + ln_rows)
            xs = x_ref[0, rows, :]
            mu = jnp.mean(xs, axis=-1, keepdims=True)
            xc = xs - mu
            var = jnp.mean(xc * xc, axis=-1, keepdims=True)
            hb = (xc * lax.rsqrt(var + LN_EPS) * scale1 + shift).astype(BF16)
            h_ref[0, rows, :] = hb
            g_ref[0, rows, :] = jnp.dot(hb, wg_ref[...], preferred_element_type=F32)
            p_ref[0, rows, :] = (jnp.dot(hb, w_ref[...], preferred_element_type=F32) * col_scale).astype(p_ref.dtype)

    @pl.when(n > 0)
    def _():
        acc = jnp.dot(h_ref[0], w_ref[...], preferred_element_type=F32)
        p_ref[0] = (acc * col_scale).astype(p_ref.dtype)


def _column_tile_map(col_tiles):
    def index_map(bi, i, j):
        tile = col_tiles[0]
        for pos in range(1, len(col_tiles)):
            tile = jnp.where(j >= pos, col_tiles[pos], tile)
        return (0, tile)
    return index_map


def _ln_projection(xin, mod, w, col_tiles, wg, tm, tn, scaled_tiles, scale, ln_rows=LN_ROW_CHUNK):
    b, t, d = xin.shape
    n = len(col_tiles) * tn
    ln_rows = min(ln_rows, tm)
    assert t % tm == 0 and tm % ln_rows == 0
    kern = functools.partial(_ln_proj_kernel, ln_rows=ln_rows, scaled_tiles=scaled_tiles, scale=scale)
    return pl.pallas_call(
        kern,
        grid=(b, t // tm, n // tn),
        in_specs=[pl.BlockSpec((1, tm, d), lambda bi, i, j: (bi, i, 0)),
                  pl.BlockSpec((1, 3, d), lambda bi, i, j: (bi, 0, 0)),
                  pl.BlockSpec((d, tn), _column_tile_map(col_tiles)),
                  pl.BlockSpec((d, LANES), lambda bi, i, j: (0, 0))],
        out_specs=[pl.BlockSpec((1, tm, tn), lambda bi, i, j: (bi, i, j)),
                   pl.BlockSpec((1, tm, LANES), lambda bi, i, j: (bi, i, 0)),
                   pl.BlockSpec((1, tm, d), lambda bi, i, j: (bi, i, 0))],
        out_shape=[jax.ShapeDtypeStruct((b, t, n), BF16),
                   jax.ShapeDtypeStruct((b, t, LANES), F32),
                   jax.ShapeDtypeStruct((b, t, d), BF16)],
        compiler_params=_params(("parallel", "parallel", "arbitrary")),
        name="ln_proj",
    )(xin, mod, w, wg)


def _act_proj_kernel(*refs, kind, scaled_tiles, scale):
    if kind == "rope":
        h_ref, w_ref, rowtab_ref, coltab_ref, o_ref = refs
    else:
        h_ref, w_ref, o_ref = refs
    acc = jnp.dot(h_ref[0], w_ref[...], preferred_element_type=F32)
    if kind == "silu":
        half = 0.5 * acc
        o_ref[0] = (half * jnp.tanh(half) + half).astype(o_ref.dtype)
    elif kind == "sigmoid":
        o_ref[0] = (0.5 * jnp.tanh(0.5 * acc) + 0.5).astype(o_ref.dtype)
    else:
        acc = acc * jnp.where(pl.program_id(2) < scaled_tiles, scale, 1.0)
        for r in range(acc.shape[0] // GRID_W):
            rows = slice(r * GRID_W, (r + 1) * GRID_W)
            for j in range(acc.shape[1] // LANES):
                lanes = slice(j * LANES, (j + 1) * LANES)
                u = acc[rows, lanes]
                if j % 2 == 0:
                    c, s = rowtab_ref[r:r + 1, :LANES], rowtab_ref[r:r + 1, LANES:]
                else:
                    c, s = coltab_ref[:, :LANES], coltab_ref[:, LANES:]
                o_ref[0, rows, lanes] = (u * c + pltpu.roll(u, LANES // 2, 1) * s).astype(o_ref.dtype)


def _act_projection(h, w, col_tiles, kind, tm, tn, tables=(), scaled_tiles=0, scale=1.0):
    b, t, d = h.shape
    n = len(col_tiles) * tn
    kern = functools.partial(_act_proj_kernel, kind=kind, scaled_tiles=scaled_tiles, scale=scale)
    in_specs = [pl.BlockSpec((1, tm, d), lambda bi, i, j: (bi, i, 0)),
                pl.BlockSpec((d, tn), _column_tile_map(col_tiles))]
    if tables:
        in_specs += [pl.BlockSpec((tm // GRID_W, 2 * LANES), lambda bi, i, j: (i, 0)),
                     pl.BlockSpec((GRID_W, 2 * LANES), lambda bi, i, j: (0, 0))]
    return pl.pallas_call(
        kern,
        grid=(b, t // tm, n // tn),
        in_specs=in_specs,
        out_specs=pl.BlockSpec((1, tm, tn), lambda bi, i, j: (bi, i, j)),
        out_shape=jax.ShapeDtypeStruct((b, t, n), BF16),
        compiler_params=_params(("parallel", "parallel", "arbitrary")),
        name="proj_" + kind,
    )(h, w, *tables)


def _na_halo_rows(rb, rows):
    return min(rb + WIN_H, rows)


def _na_halo_start(blk, rb, rows):
    return jnp.clip(blk * rb - WIN_H // 2, 0, rows - _na_halo_rows(rb, rows))


def _na_kernel(q_ref, k_ref, v_ref, kc_ref, vc_ref, bias_ref, o_ref, s_buf, p_buf, l_buf, *, rb, rows):
    blk = pl.program_id(2)
    hd = NA_HEAD_DIM
    n_loc = WIN_H * GRID_W
    is_a = lax.broadcasted_iota(jnp.int32, (GRID_W, 2 * hd), 1) < hd
    contract_last = (((1,), (1,)), ((), ()))

    halo_start = _na_halo_start(blk, rb, rows)

    def window_start(i):
        r = blk * rb + i
        r_start = jnp.clip(r - WIN_H // 2, 0, rows - WIN_H)
        return r_start - r + (WIN_H - 1), pl.multiple_of((r_start - halo_start) * GRID_W, GRID_W)

    def scores(i):
        cls, k0 = window_start(i)
        q = q_ref[0, i * GRID_W:(i + 1) * GRID_W, :]
        zero = jnp.zeros_like(q)
        qs = jnp.concatenate([jnp.where(is_a, q, zero), jnp.where(is_a, zero, q)], axis=0)
        k8 = k_ref[0, pl.ds(k0, n_loc), :]
        bias = jnp.concatenate([bias_ref[0, cls + 2 * jj] for jj in range(WIN_H // 2)], axis=1)
        s_buf[i % 2, :, :n_loc] = lax.dot_general(qs, k8, contract_last, preferred_element_type=F32) + bias
        s_buf[i % 2, :, n_loc:] = lax.dot_general(qs, kc_ref[0], contract_last, preferred_element_type=F32)

    def softmax(i):
        s = s_buf[i % 2]
        m = jnp.max(s, axis=-1, keepdims=True)
        p = jnp.exp(s - m)
        l_buf[i % 2] = jnp.sum(p, axis=-1, keepdims=True)
        p_buf[i % 2] = p.astype(BF16)

    def values(i):
        _, k0 = window_start(i)
        v8 = v_ref[0, pl.ds(k0, n_loc), :]
        o = (jnp.dot(p_buf[i % 2, :, :n_loc], v8, preferred_element_type=F32)
             + jnp.dot(p_buf[i % 2, :, n_loc:], vc_ref[0], preferred_element_type=F32))
        o = o / l_buf[i % 2]
        out = jnp.where(is_a, o[:GRID_W], o[GRID_W:])
        o_ref[0, i * GRID_W:(i + 1) * GRID_W, :] = out.astype(o_ref.dtype)

    for t in range(rb + 2):
        if t >= 2:
            values(t - 2)
        if 1 <= t <= rb:
            softmax(t - 1)
        if t < rb:
            scores(t)


def _neighbourhood_attention(p, p_c, bias, rb=NA_ROWS_PER_STEP):
    b, s, _ = p.shape
    lc = p_c.shape[1]
    rows = s // GRID_W
    rb = min(rb, rows)
    assert s % GRID_W == 0 and rows % rb == 0 and rows >= WIN_H
    hg = NA_WIDTH // LANES
    kern = functools.partial(_na_kernel, rb=rb, rows=rows)
    n_keys = WIN_H * GRID_W + lc
    halo_tokens = _na_halo_rows(rb, rows) * GRID_W

    def halo_spec(first_group):
        return pl.BlockSpec(
            (pl.Element(1), pl.Element(halo_tokens), pl.Element(LANES)),
            lambda bi, g, i: (bi, _na_halo_start(i, rb, rows) * GRID_W, (first_group + g) * LANES))

    return pl.pallas_call(
        kern,
        grid=(b, hg, rows // rb),
        in_specs=[pl.BlockSpec((1, rb * GRID_W, LANES), lambda bi, g, i: (bi, i, g)),
                  halo_spec(hg), halo_spec(2 * hg),
                  pl.BlockSpec((1, lc, LANES), lambda bi, g, i: (bi, 0, hg + g)),
                  pl.BlockSpec((1, lc, LANES), lambda bi, g, i: (bi, 0, 2 * hg + g)),
                  pl.BlockSpec((1, 2 * WIN_H - 2, 2 * GRID_W, 2 * GRID_W), lambda bi, g, i: (g, 0, 0, 0))],
        out_specs=pl.BlockSpec((1, rb * GRID_W, LANES), lambda bi, g, i: (bi, i, g)),
        out_shape=jax.ShapeDtypeStruct((b, s, NA_WIDTH), BF16),
        scratch_shapes=[pltpu.VMEM((2, 2 * GRID_W, n_keys), F32),
                        pltpu.VMEM((2, 2 * GRID_W, n_keys), BF16),
                        pltpu.VMEM((2, 2 * GRID_W, 1), F32)],
        compiler_params=_params(("parallel", "parallel", "arbitrary")),
        name="na",
    )(p, p, p, p_c, p_c, bias)


def _na_bias_table(rpb):
    qcol = jnp.arange(GRID_W)[:, None]
    kcol = jnp.arange(GRID_W)[None, :]
    c_start = jnp.clip(qcol - WIN_W // 2, 0, GRID_W - WIN_W)
    in_win = (kcol >= c_start) & (kcol < c_start + WIN_W)
    ring = 2 * GRID_W
    fill = jnp.full(rpb.shape[:2] + (ring - (2 * WIN_W - 1),), MASK_VALUE, rpb.dtype)
    ext = jnp.concatenate([rpb[:, :, WIN_W - 1:], fill, rpb[:, :, :WIN_W - 1]], axis=-1)
    toep = jnp.tile(ext, (1, 1, GRID_W))[:, :, :GRID_W * (ring - 1)]
    toep = toep.reshape(rpb.shape[:2] + (GRID_W, ring - 1))[..., :GRID_W]
    band = jnp.where(in_win[None, None], toep, MASK_VALUE).astype(F32)
    n_dr = band.shape[1]
    band = band.reshape(NA_HEADS // 2, 2, n_dr, GRID_W, GRID_W).transpose(0, 2, 1, 3, 4)
    band = band.reshape(NA_HEADS // 2, n_dr, 2 * GRID_W, GRID_W)
    return jnp.concatenate([band[:, :-1], band[:, 1:]], axis=-1)


def _gla_kernel(*refs, reverse, has_init, emit_o, combine, emit_state):
    refs = list(refs)
    q_ref, k_ref, v_ref, lr_ref, w2_ref, b2_ref = refs[:6]
    pos = 6
    s0_ref = ob_ref = ng_ref = out_ref = sfin_ref = None
    if has_init:
        s0_ref = refs[pos]; pos += 1
    if combine:
        ob_ref, ng_ref = refs[pos], refs[pos + 1]; pos += 2
    if emit_o:
        out_ref = refs[pos]; pos += 1
    if emit_state:
        sfin_ref = refs[pos]; pos += 1
    s_ref, cum_buf, kend_buf, dec_buf, att_buf, qin_buf = refs[pos:pos + 6]

    j = pl.program_id(2)
    nblk = pl.num_programs(2)
    blk_len = GLA_BLOCK
    half = blk_len // 2
    c_len = GLA_CHUNK

    @pl.when(j == 0)
    def _():
        if has_init:
            s_ref[...] = s0_ref[0, 0]
        else:
            s_ref[...] = jnp.zeros_like(s_ref)

    ri = lax.broadcasted_iota(jnp.int32, (blk_len, blk_len), 0)
    ci = lax.broadcasted_iota(jnp.int32, (blk_len, blk_len), 1)
    tri = (ci >= ri) if reverse else (ci <= ri)
    tri_b = tri.astype(BF16)
    tri_h = tri[:half, :half]
    contract_last = (((1,), (1,)), ((), ()))
    contract_first = (((0,), (0,)), ((), ()))

    def bcast_halves(lo_row, hi_row):
        return jnp.concatenate([jnp.broadcast_to(lo_row, (half, lo_row.shape[1])),
                                jnp.broadcast_to(hi_row, (half, hi_row.shape[1]))], axis=0)

    def decay(r0):
        rows = slice(r0, r0 + blk_len)
        z = jnp.dot(lr_ref[0, rows, :].astype(BF16), w2_ref[0], preferred_element_type=F32) + b2_ref[0]
        g = (jnp.minimum(z, 0.0) - jnp.log(1.0 + jnp.exp(-jnp.abs(z)))) * (1.0 / GATE_NORMALIZER)
        g_hi = g.astype(BF16)
        g_lo = (g - g_hi.astype(F32)).astype(BF16)
        cum_buf[rows, :] = (jnp.dot(tri_b, g_hi, preferred_element_type=F32)
                            + jnp.dot(tri_b, g_lo, preferred_element_type=F32))

    def prepare(r0, slot):
        rows = slice(r0, r0 + blk_len)
        cum = cum_buf[rows, :]

        def row(i):
            return cum_buf[r0 + i:r0 + i + 1, :]

        if reverse:
            m_lo, m_hi, mid, end = row(c_len), row(half + c_len), row(half), row(0)
        else:
            m_lo, m_hi, mid, end = row(c_len - 1), row(half + c_len - 1), row(half - 1), row(blk_len - 1)
        m_ref = bcast_halves(m_lo, m_hi)
        def expb(a):
            return jnp.exp(a).astype(BF16)

        km = k_ref[0, rows, :] * expb(m_ref - cum)
        kend_buf[slot] = km * bcast_halves(expb(end - m_lo), expb(end - m_hi))
        dec_buf[slot] = jnp.exp(end)
        if emit_o:
            qm = q_ref[0, rows, :] * expb(cum - m_ref)
            a_lo = lax.dot_general(qm[:half], km[:half], contract_last, preferred_element_type=F32)
            a_hi = lax.dot_general(qm[half:], km[half:], contract_last, preferred_element_type=F32)
            a_lo = jnp.where(tri_h, a_lo, 0.0)
            a_hi = jnp.where(tri_h, a_hi, 0.0)
            zeros = jnp.zeros((half, half), F32)
            if reverse:
                a_off = lax.dot_general(qm[:half] * expb(m_lo - mid), km[half:] * expb(mid - m_hi),
                                        contract_last, preferred_element_type=F32)
                att = jnp.concatenate([jnp.concatenate([a_lo, a_off], axis=1),
                                       jnp.concatenate([zeros, a_hi], axis=1)], axis=0)
            else:
                a_off = lax.dot_general(qm[half:] * expb(m_hi - mid), km[:half] * expb(mid - m_lo),
                                        contract_last, preferred_element_type=F32)
                att = jnp.concatenate([jnp.concatenate([a_lo, zeros], axis=1),
                                       jnp.concatenate([a_off, a_hi], axis=1)], axis=0)
            att_buf[slot] = att.astype(BF16)
            qin_buf[slot] = qm * bcast_halves(expb(m_lo), expb(m_hi))

    def advance(r0, slot):
        rows = slice(r0, r0 + blk_len)
        v = v_ref[0, rows, :]
        st = s_ref[...]
        if emit_o:
            o = (jnp.dot(att_buf[slot], v, preferred_element_type=F32)
                 + lax.dot_general(qin_buf[slot], st.astype(BF16), contract_last, preferred_element_type=F32))
            if combine:
                tot = o + ob_ref[0, rows, :].astype(F32)
                ms = jnp.mean(tot * tot, axis=-1, keepdims=True)
                o = tot * lax.rsqrt(ms + RMS_EPS) * ng_ref[...]
            out_ref[0, rows, :] = o.astype(out_ref.dtype)
        s_ref[...] = st * dec_buf[slot] + lax.dot_general(v, kend_buf[slot], contract_first,
                                                          preferred_element_type=F32)

    nb = q_ref.shape[1] // blk_len
    order = list(range(nb - 1, -1, -1) if reverse else range(nb))
    for bi in order:
        decay(bi * blk_len)
    prepare(order[0] * blk_len, 0)
    for n, bi in enumerate(order):
        if n + 1 < nb:
            prepare(order[n + 1] * blk_len, (n + 1) % 2)
        advance(bi * blk_len, n % 2)

    if emit_state:
        @pl.when(j == nblk - 1)
        def _():
            sfin_ref[0, 0] = s_ref[...]


def _gla_scan(p_qk, p_v, g_lr, w2e, b2e, *, direction, tb, q_blk, k_blk, v_blk, dk, dv,
              s0=None, ob=None, norm_g=None, emit_o=True, emit_state=False):
    b, t, _ = p_qk.shape
    nblk = t // tb
    reverse = direction == 1
    combine = ob is not None
    has_init = s0 is not None

    def blk(j):
        return (nblk - 1 - j) if reverse else j

    in_specs = [pl.BlockSpec((1, tb, dk), lambda bi, h, j: (bi, blk(j), q_blk + h)),
                pl.BlockSpec((1, tb, dk), lambda bi, h, j: (bi, blk(j), k_blk + h)),
                pl.BlockSpec((1, tb, dv), lambda bi, h, j: (bi, blk(j), v_blk + h)),
                pl.BlockSpec((1, tb, LANES), lambda bi, h, j: (bi, blk(j), 0)),
                pl.BlockSpec((1, LANES, dk), lambda bi, h, j: (direction, 0, h)),
                pl.BlockSpec((1, 1, dk), lambda bi, h, j: (direction, 0, h))]
    args = [p_qk, p_qk, p_v, g_lr, w2e, b2e]
    if has_init:
        in_specs.append(pl.BlockSpec((1, 1, dv, dk), lambda bi, h, j: (bi, h, 0, 0)))
        args.append(s0)
    if combine:
        in_specs.append(pl.BlockSpec((1, tb, dv), lambda bi, h, j: (bi, blk(j), h)))
        in_specs.append(pl.BlockSpec((1, dv), lambda bi, h, j: (0, 0)))
        args += [ob, norm_g]
    out_specs, out_shape = [], []
    if emit_o:
        out_specs.append(pl.BlockSpec((1, tb, dv), lambda bi, h, j: (bi, blk(j), h)))
        out_shape.append(jax.ShapeDtypeStruct((b, t, GLA_HEADS * dv), BF16))
    if emit_state:
        out_specs.append(pl.BlockSpec((1, 1, dv, dk), lambda bi, h, j: (bi, h, 0, 0)))
        out_shape.append(jax.ShapeDtypeStruct((b, GLA_HEADS, dv, dk), F32))
    assert tb % GLA_BLOCK == 0
    kern = functools.partial(_gla_kernel, reverse=reverse, has_init=has_init,
                             emit_o=emit_o, combine=combine, emit_state=emit_state)
    res = pl.pallas_call(
        kern,
        grid=(b, GLA_HEADS, nblk),
        in_specs=in_specs,
        out_specs=out_specs,
        out_shape=out_shape,
        scratch_shapes=[pltpu.VMEM((dv, dk), F32),
                        pltpu.VMEM((tb, dk), F32),
                        pltpu.VMEM((2, GLA_BLOCK, dk), BF16),
                        pltpu.VMEM((2, 1, dk), F32),
                        pltpu.VMEM((2, GLA_BLOCK, GLA_BLOCK), BF16),
                        pltpu.VMEM((2, GLA_BLOCK, dk), BF16)],
        compiler_params=_params(("parallel", "parallel", "arbitrary")),
        name="gla_" + ("bwd" if reverse else "fwd") + ("_ctx" if not emit_o else ""),
    )(*args)
    return res[0] if len(res) == 1 else res


def _branch_merge_kernel(ya_ref, az_ref, yb_ref, bz_ref, ga_ref, gb_ref, wa_ref, wb_ref, m_ref, *, chunk):
    for r0 in range(0, ya_ref.shape[1], chunk):
        rows = slice(r0, r0 + chunk)
        ua = ya_ref[0, rows, :] * az_ref[0, rows, :]
        ub = yb_ref[0, rows, :] * bz_ref[0, rows, :]
        pa = jnp.dot(ua, wa_ref[...], preferred_element_type=F32)
        pb = jnp.dot(ub, wb_ref[...], preferred_element_type=F32)
        m_ref[0, rows, :] = (ga_ref[0, rows, :].astype(F32) * pa
                             + gb_ref[0, rows, :].astype(F32) * pb).astype(m_ref.dtype)


def _output_kernel(m_ref, x_ref, mod_ref, wo_ref, lng_ref, lnb_ref, o_ref, *, chunk):
    for r0 in range(0, m_ref.shape[1], chunk):
        rows = slice(r0, r0 + chunk)
        out = jnp.dot(m_ref[0, rows, :], wo_ref[...], preferred_element_type=F32)
        z = DEEPNORM_ALPHA * x_ref[0, rows, :] + mod_ref[0, 2:3, :] * out
        mu = jnp.mean(z, axis=-1, keepdims=True)
        zc = z - mu
        var = jnp.mean(zc * zc, axis=-1, keepdims=True)
        o_ref[0, rows, :] = zc * lax.rsqrt(var + LN_EPS) * lng_ref[...] + lnb_ref[...]


def _merge(y_a, y_b, p_silu, p_gate, x, mod, wa, wb, wo, ln_g, ln_b, az_blk, bz_blk, ga_blk, gb_blk,
           tm=MERGE_ROW_TILE, chunk=MERGE_ROW_CHUNK):
    b, s, d = x.shape
    na = y_a.shape[2]
    dv = y_b.shape[2]
    tm = min(tm, s)
    chunk = min(chunk, tm)
    assert s % tm == 0 and tm % chunk == 0
    const = dict(pipeline_mode=pl.Buffered(1))
    merged = pl.pallas_call(
        functools.partial(_branch_merge_kernel, chunk=chunk),
        grid=(b, s // tm),
        in_specs=[pl.BlockSpec((1, tm, na), lambda bi, i: (bi, i, 0)),
                  pl.BlockSpec((1, tm, na), lambda bi, i: (bi, i, az_blk)),
                  pl.BlockSpec((1, tm, dv), lambda bi, i: (bi, i, 0)),
                  pl.BlockSpec((1, tm, dv), lambda bi, i: (bi, i, bz_blk)),
                  pl.BlockSpec((1, tm, d), lambda bi, i: (bi, i, ga_blk)),
                  pl.BlockSpec((1, tm, d), lambda bi, i: (bi, i, gb_blk)),
                  pl.BlockSpec((na, d), lambda bi, i: (0, 0), **const),
                  pl.BlockSpec((dv, d), lambda bi, i: (0, 0), **const)],
        out_specs=pl.BlockSpec((1, tm, d), lambda bi, i: (bi, i, 0)),
        out_shape=jax.ShapeDtypeStruct((b, s, d), BF16),
        compiler_params=_params(("parallel", "parallel")),
        name="branch_merge",
    )(y_a, p_silu, y_b, p_silu, p_gate, p_gate, wa, wb)
    return pl.pallas_call(
        functools.partial(_output_kernel, chunk=chunk),
        grid=(b, s // tm),
        in_specs=[pl.BlockSpec((1, tm, d), lambda bi, i: (bi, i, 0)),
                  pl.BlockSpec((1, tm, d), lambda bi, i: (bi, i, 0)),
                  pl.BlockSpec((1, 3, d), lambda bi, i: (bi, 0, 0)),
                  pl.BlockSpec((d, d), lambda bi, i: (0, 0), **const),
                  pl.BlockSpec((1, d), lambda bi, i: (0, 0)),
                  pl.BlockSpec((1, d), lambda bi, i: (0, 0))],
        out_specs=pl.BlockSpec((1, tm, d), lambda bi, i: (bi, i, 0)),
        out_shape=jax.ShapeDtypeStruct((b, s, d), F32),
        compiler_params=_params(("parallel", "parallel")),
        name="output",
    )(merged, x, mod, wo, ln_g, ln_b)


def _rope_tables(s):
    pairs = LANES // 2
    inv_freq = ROPE_BASE ** (-jnp.arange(pairs, dtype=F32) / pairs)

    def table(n):
        ang = jnp.arange(n).astype(F32)[:, None] * inv_freq[None, :]
        return jnp.concatenate([jnp.cos(ang), jnp.cos(ang), -jnp.sin(ang), jnp.sin(ang)], axis=-1)

    return table(s // GRID_W), table(GRID_W)


def _identity_rope_tables(s):
    def table(n):
        return jnp.concatenate([jnp.ones((n, LANES), F32), jnp.zeros((n, LANES), F32)], axis=-1)

    return table(s // GRID_W), table(GRID_W)


def _expand_gate(w_gate2, b_gate, dk):
    dirs, rank, _ = w_gate2.shape
    pairs = dk // 4

    def dup(a):
        a = a.reshape(a.shape[:-1] + (GLA_HEADS, 2, pairs))
        a = jnp.concatenate([a, a], axis=-1)
        return a.reshape(a.shape[:-3] + (GLA_HEADS * dk,))

    w = dup(w_gate2.astype(F32))
    w2e = jnp.zeros((dirs, LANES, GLA_HEADS * dk), F32)
    for d in range(dirs):
        w2e = w2e.at[d, d * rank:(d + 1) * rank].set(w[d])
    b2e = dup(b_gate.astype(F32))[:, None, :]
    return w2e, b2e


def kernel(x, c, ctx, c_ctx, w_mod, b_mod, w_in, na_rpb, gla_w_gate2, gla_b_gate, gla_norm_g,
           w_br_a, w_br_b, w_out, ln_g, ln_b):
    bsz, s, d = x.shape
    lc = ctx.shape[1]
    key_dim, val_dim = d // 2, d
    dk, dv = key_dim // GLA_HEADS, val_dim // GLA_HEADS
    lyr = 0

    widths = (NA_WIDTH,) * 4 + (key_dim, key_dim, val_dim, val_dim, N_DIRS * GATE_RANK, d, d)
    offs = [0]
    for w_ in widths:
        offs.append(offs[-1] + w_)
    o_aq, o_ak, o_av, o_az, o_bq, o_bk, o_bv, o_bz, o_bg, o_ga, o_gb, o_end = offs
    tn = PROJ_COL_TILE
    tm = min(LN_ROW_TILE, s)
    assert all(o % tn == 0 for o in offs[:9]) and (o_end - o_ga) % tn == 0
    w_full = w_in[lyr]
    w_head = w_full.astype(BF16)
    w_sig = w_head[:, o_ga:o_end]
    w_gate1 = jnp.pad(w_head[:, o_bg:o_ga], ((0, 0), (0, LANES - (o_ga - o_bg))))

    def tiles(*ranges):
        return tuple(t for lo, hi in ranges for t in range(lo // tn, hi // tn))

    lin_tiles = tiles((o_aq, o_az), (o_bv, o_bz))
    rot_tiles = tiles((o_bq, o_bv))
    silu_tiles = tiles((o_bz, o_bg), (o_az, o_bq))
    sig_tiles = tiles((0, o_end - o_ga))

    cs = jnp.concatenate([c, c_ctx[None], jnp.zeros((MOD_ROWS - bsz - 1, d), F32)], axis=0)
    mod = _modulation(cs, w_mod[lyr], b_mod[lyr][None])
    mod_lat = mod[:bsz].reshape(bsz, 3, d)
    mod_ctx = mod[bsz].reshape(1, 3, d)

    na_scale = dict(scaled_tiles=NA_WIDTH // tn, scale=NA_HEAD_DIM ** -0.5)
    rot_scale = dict(scaled_tiles=key_dim // tn, scale=dk ** -0.5)
    p_lin, g_lr, h = _ln_projection(x, mod_lat, w_head, lin_tiles, w_gate1, tm, tn, **na_scale)
    tm_act = min(ACT_ROW_TILE, s)
    p_rot = _act_projection(h, w_head, rot_tiles, "rope", tm_act, tn, tables=_rope_tables(s), **rot_scale)
    p_silu = _act_projection(h, w_head, silu_tiles, "silu", tm_act, tn)
    p_sig = _act_projection(h, w_sig, sig_tiles, "sigmoid", tm_act, tn)
    ctx_rows = ctx.reshape(1, bsz * lc, d)
    p_lin_c, g_lr_c, h_c = _ln_projection(ctx_rows, mod_ctx, w_head, lin_tiles, w_gate1, bsz * lc, tn, **na_scale)
    p_rot_c = _act_projection(h_c, w_head, rot_tiles, "rope", bsz * lc, tn,
                              tables=_identity_rope_tables(bsz * lc), **rot_scale)
    p_lin_c, g_lr_c, p_rot_c = (a.reshape(bsz, lc, a.shape[-1]) for a in (p_lin_c, g_lr_c, p_rot_c))

    y_a = _neighbourhood_attention(p_lin, p_lin_c, _na_bias_table(na_rpb[lyr]))

    w2e, b2e = _expand_gate(gla_w_gate2[lyr], gla_b_gate[lyr], dk)
    w2e = w2e.astype(BF16)
    blocks = dict(q_blk=0, k_blk=key_dim // dk, v_blk=3 * NA_WIDTH // dv, dk=dk, dv=dv)
    s_f = _gla_scan(p_rot_c, p_lin_c, g_lr_c, w2e, b2e, direction=0, tb=lc, emit_o=False, emit_state=True,
                    **blocks)
    s_b = _gla_scan(p_rot_c, p_lin_c, g_lr_c, w2e, b2e, direction=1, tb=lc, emit_o=False, emit_state=True,
                    **blocks)
    tb = min(GLA_ROWS_PER_STEP, s)
    o_b = _gla_scan(p_rot, p_lin, g_lr, w2e, b2e, direction=1, tb=tb, s0=s_b, **blocks)
    y_b = _gla_scan(p_rot, p_lin, g_lr, w2e, b2e, direction=0, tb=tb, s0=s_f, ob=o_b,
                    norm_g=gla_norm_g[lyr][None].astype(F32), **blocks)

    return _merge(y_a, y_b, p_silu, p_sig, x, mod_lat,
                  w_br_a[lyr].astype(BF16), w_br_b[lyr].astype(BF16), w_out[lyr].astype(BF16),
                  ln_g[lyr][None], ln_b[lyr][None],
                  az_blk=val_dim // NA_WIDTH, bz_blk=0, ga_blk=0, gb_blk=1)
```

```python
import functools

import jax
import jax.numpy as jnp
from jax import lax
from jax.experimental import pallas as pl
from jax.experimental.pallas import tpu as pltpu

F32 = jnp.float32
BF16 = jnp.bfloat16

GRID_W = 64
WIN_H = 8
WIN_W = 16
NA_HEADS = 16
NA_HEAD_DIM = 64
NA_WIDTH = NA_HEADS * NA_HEAD_DIM
GLA_HEADS = 4
GATE_RANK = 16
GATE_NORMALIZER = 16.0
GLA_CHUNK = 64
GLA_BLOCK = 4 * GLA_CHUNK
N_DIRS = 2
ROPE_BASE = 10000.0
DEPTH = 1
DEEPNORM_ALPHA = (2 * DEPTH) ** 0.25
LN_EPS = 1e-6
RMS_EPS = 1e-6
MASK_VALUE = -1e30
LOG2_E = 1.4426950408889634

LANES = 128
MOD_ROWS = 8
VMEM_LIMIT = 56 * 2 ** 20

MOD_COL_TILE = 768
PROJ_COL_TILE = 1024
LN_ROW_TILE = 1024
LN_ROW_CHUNK = 256
ACT_ROW_TILE = 2048
NA_ROWS_PER_STEP = 64
GLA_ROWS_PER_STEP = 4096
MERGE_ROW_TILE = 512
MERGE_ROW_CHUNK = 256


def _params(sem, vmem=VMEM_LIMIT):
    return pltpu.CompilerParams(dimension_semantics=sem, vmem_limit_bytes=vmem)


def _mod_kernel(c_ref, w_ref, b_ref, o_ref):
    c = c_ref[...]
    s = c * jax.nn.sigmoid(c)
    o_ref[...] = jnp.dot(s, w_ref[...], preferred_element_type=F32) + b_ref[...]


def _modulation(cs, w, b, tn=MOD_COL_TILE):
    rows, d = cs.shape
    n = w.shape[1]
    assert n % tn == 0
    return pl.pallas_call(
        _mod_kernel,
        grid=(n // tn,),
        in_specs=[pl.BlockSpec((rows, d), lambda j: (0, 0)),
                  pl.BlockSpec((d, tn), lambda j: (0, j)),
                  pl.BlockSpec((1, tn), lambda j: (0, j))],
        out_specs=pl.BlockSpec((rows, tn), lambda j: (0, j)),
        out_shape=jax.ShapeDtypeStruct((rows, n), F32),
        compiler_params=_params(("parallel",)),
        name="mod",
    )(cs, w, b)


def _ln_proj_kernel(x_ref, mod_ref, w_ref, wg_ref, p_ref, g_ref, h_ref, *, ln_rows, scaled_tiles, scale):
    n = pl.program_id(2)
    tm = x_ref.shape[1]
    col_scale = jnp.where(n < scaled_tiles, scale, 1.0)

    @pl.when(n == 0)
    def _():
        shift = mod_ref[0, 0:1, :]
        scale1 = 1.0 + mod_ref[0, 1:2, :]
        for r0 in range(0, tm, ln_rows):
            rows = slice(r0, r0 + ln_rows)
            xs = x_ref[0, rows, :]
            mu = jnp.mean(xs, axis=-1, keepdims=True)
            xc = xs - mu
            var = jnp.mean(xc * xc, axis=-1, keepdims=True)
            hb = (xc * lax.rsqrt(var + LN_EPS) * scale1 + shift).astype(BF16)
            h_ref[0, rows, :] = hb
            g_ref[0, rows, :] = jnp.dot(hb, wg_ref[...], preferred_element_type=F32)
            p_ref[0, rows, :] = (jnp.dot(hb, w_ref[...], preferred_element_type=F32) * col_scale).astype(p_ref.dtype)

    @pl.when(n > 0)
    def _():
        acc = jnp.dot(h_ref[0], w_ref[...], preferred_element_type=F32)
        p_ref[0] = (acc * col_scale).astype(p_ref.dtype)


def _column_tile_map(col_tiles):
    def index_map(bi, i, j):
        tile = col_tiles[0]
        for pos in range(1, len(col_tiles)):
            tile = jnp.where(j >= pos, col_tiles[pos], tile)
        return (0, tile)
    return index_map


def _ln_projection(xin, mod, w, col_tiles, wg, tm, tn, scaled_tiles, scale, ln_rows=LN_ROW_CHUNK):
    b, t, d = xin.shape
    n = len(col_tiles) * tn
    ln_rows = min(ln_rows, tm)
    assert t % tm == 0 and tm % ln_rows == 0
    kern = functools.partial(_ln_proj_kernel, ln_rows=ln_rows, scaled_tiles=scaled_tiles, scale=scale)
    return pl.pallas_call(
        kern,
        grid=(b, t // tm, n // tn),
        in_specs=[pl.BlockSpec((1, tm, d), lambda bi, i, j: (bi, i, 0)),
                  pl.BlockSpec((1, 3, d), lambda bi, i, j: (bi, 0, 0)),
                  pl.BlockSpec((d, tn), _column_tile_map(col_tiles)),
                  pl.BlockSpec((d, LANES), lambda bi, i, j: (0, 0))],
        out_specs=[pl.BlockSpec((1, tm, tn), lambda bi, i, j: (bi, i, j)),
                   pl.BlockSpec((1, tm, LANES), lambda bi, i, j: (bi, i, 0)),
                   pl.BlockSpec((1, tm, d), lambda bi, i, j: (bi, i, 0))],
        out_shape=[jax.ShapeDtypeStruct((b, t, n), BF16),
                   jax.ShapeDtypeStruct((b, t, LANES), F32),
                   jax.ShapeDtypeStruct((b, t, d), BF16)],
        compiler_params=_params(("parallel", "parallel", "arbitrary")),
        name="ln_proj",
    )(xin, mod, w, wg)


def _act_proj_kernel(*refs, kind, scaled_tiles, scale):
    if kind == "rope":
        h_ref, w_ref, rowtab_ref, coltab_ref, o_ref = refs
    else:
        h_ref, w_ref, o_ref = refs
    acc = jnp.dot(h_ref[0], w_ref[...], preferred_element_type=F32)
    if kind == "silu":
        half = 0.5 * acc
        o_ref[0] = (half * jnp.tanh(half) + half).astype(o_ref.dtype)
    elif kind == "sigmoid":
        o_ref[0] = (0.5 * jnp.tanh(0.5 * acc) + 0.5).astype(o_ref.dtype)
    else:
        acc = acc * jnp.where(pl.program_id(2) < scaled_tiles, scale, 1.0)
        for r in range(acc.shape[0] // GRID_W):
            rows = slice(r * GRID_W, (r + 1) * GRID_W)
            for j in range(acc.shape[1] // LANES):
                lanes = slice(j * LANES, (j + 1) * LANES)
                u = acc[rows, lanes]
                if j % 2 == 0:
                    c, s = rowtab_ref[r:r + 1, :LANES], rowtab_ref[r:r + 1, LANES:]
                else:
                    c, s = coltab_ref[:, :LANES], coltab_ref[:, LANES:]
                o_ref[0, rows, lanes] = (u * c + pltpu.roll(u, LANES // 2, 1) * s).astype(o_ref.dtype)


def _act_projection(h, w, col_tiles, kind, tm, tn, tables=(), scaled_tiles=0, scale=1.0):
    b, t, d = h.shape
    n = len(col_tiles) * tn
    kern = functools.partial(_act_proj_kernel, kind=kind, scaled_tiles=scaled_tiles, scale=scale)
    in_specs = [pl.BlockSpec((1, tm, d), lambda bi, i, j: (bi, i, 0)),
                pl.BlockSpec((d, tn), _column_tile_map(col_tiles))]
    if tables:
        in_specs += [pl.BlockSpec((tm // GRID_W, 2 * LANES), lambda bi, i, j: (i, 0)),
                     pl.BlockSpec((GRID_W, 2 * LANES), lambda bi, i, j: (0, 0))]
    return pl.pallas_call(
        kern,
        grid=(b, t // tm, n // tn),
        in_specs=in_specs,
        out_specs=pl.BlockSpec((1, tm, tn), lambda bi, i, j: (bi, i, j)),
        out_shape=jax.ShapeDtypeStruct((b, t, n), BF16),
        compiler_params=_params(("parallel", "parallel", "arbitrary")),
        name="proj_" + kind,
    )(h, w, *tables)


def _na_halo_rows(rb, rows):
    return min(rb + WIN_H, rows)


def _na_halo_start(blk, rb, rows):
    return jnp.clip(blk * rb - WIN_H // 2, 0, rows - _na_halo_rows(rb, rows))


def _na_kernel(q_ref, k_ref, v_ref, kc_ref, vc_ref, bias_ref, o_ref, s_buf, p_buf, l_buf, *, rb, rows):
    blk = pl.program_id(2)
    hd = NA_HEAD_DIM
    n_loc = WIN_H * GRID_W
    is_a = lax.broadcasted_iota(jnp.int32, (GRID_W, 2 * hd), 1) < hd
    contract_last = (((1,), (1,)), ((), ()))

    halo_start = _na_halo_start(blk, rb, rows)

    def window_start(i):
        r = blk * rb + i
        r_start = jnp.clip(r - WIN_H // 2, 0, rows - WIN_H)
        return r_start - r + (WIN_H - 1), pl.multiple_of((r_start - halo_start) * GRID_W, GRID_W)

    def scores(i):
        cls, k0 = window_start(i)
        q = q_ref[0, i * GRID_W:(i + 1) * GRID_W, :]
        zero = jnp.zeros_like(q)
        qs = jnp.concatenate([jnp.where(is_a, q, zero), jnp.where(is_a, zero, q)], axis=0)
        k8 = k_ref[0, pl.ds(k0, n_loc), :]
        bias = jnp.concatenate([bias_ref[0, cls + 2 * jj] for jj in range(WIN_H // 2)], axis=1)
        s_buf[i % 2, :, :n_loc] = lax.dot_general(qs, k8, contract_last, preferred_element_type=F32) + bias
        s_buf[i % 2, :, n_loc:] = lax.dot_general(qs, kc_ref[0], contract_last, preferred_element_type=F32)

    def softmax(i):
        s = s_buf[i % 2]
        m = jnp.max(s, axis=-1, keepdims=True)
        p = jnp.exp2(s - m)
        l_buf[i % 2] = jnp.sum(p, axis=-1, keepdims=True)
        p_buf[i % 2] = p.astype(BF16)

    def values(i):
        _, k0 = window_start(i)
        v8 = v_ref[0, pl.ds(k0, n_loc), :]
        o = (jnp.dot(p_buf[i % 2, :, :n_loc], v8, preferred_element_type=F32)
             + jnp.dot(p_buf[i % 2, :, n_loc:], vc_ref[0], preferred_element_type=F32))
        o = o / l_buf[i % 2]
        out = jnp.where(is_a, o[:GRID_W], o[GRID_W:])
        o_ref[0, i * GRID_W:(i + 1) * GRID_W, :] = out.astype(o_ref.dtype)

    for t in range(rb + 2):
        if t >= 2:
            values(t - 2)
        if 1 <= t <= rb:
            softmax(t - 1)
        if t < rb:
            scores(t)


def _neighbourhood_attention(p, p_c, bias, rb=NA_ROWS_PER_STEP):
    b, s, _ = p.shape
    lc = p_c.shape[1]
    rows = s // GRID_W
    rb = min(rb, rows)
    assert s % GRID_W == 0 and rows % rb == 0 and rows >= WIN_H
    hg = NA_WIDTH // LANES
    kern = functools.partial(_na_kernel, rb=rb, rows=rows)
    n_keys = WIN_H * GRID_W + lc
    halo_tokens = _na_halo_rows(rb, rows) * GRID_W

    def halo_spec(first_group):
        return pl.BlockSpec(
            (pl.Element(1), pl.Element(halo_tokens), pl.Element(LANES)),
            lambda bi, g, i: (bi, _na_halo_start(i, rb, rows) * GRID_W, (first_group + g) * LANES))

    return pl.pallas_call(
        kern,
        grid=(b, hg, rows // rb),
        in_specs=[pl.BlockSpec((1, rb * GRID_W, LANES), lambda bi, g, i: (bi, i, g)),
                  halo_spec(hg), halo_spec(2 * hg),
                  pl.BlockSpec((1, lc, LANES), lambda bi, g, i: (bi, 0, hg + g)),
                  pl.BlockSpec((1, lc, LANES), lambda bi, g, i: (bi, 0, 2 * hg + g)),
                  pl.BlockSpec((1, 2 * WIN_H - 2, 2 * GRID_W, 2 * GRID_W), lambda bi, g, i: (g, 0, 0, 0))],
        out_specs=pl.BlockSpec((1, rb * GRID_W, LANES), lambda bi, g, i: (bi, i, g)),
        out_shape=jax.ShapeDtypeStruct((b, s, NA_WIDTH), BF16),
        scratch_shapes=[pltpu.VMEM((2, 2 * GRID_W, n_keys), F32),
                        pltpu.VMEM((2, 2 * GRID_W, n_keys), BF16),
                        pltpu.VMEM((2, 2 * GRID_W, 1), F32)],
        compiler_params=_params(("parallel", "parallel", "arbitrary")),
        name="na",
    )(p, p, p, p_c, p_c, bias)


def _na_bias_table(rpb):
    qcol = jnp.arange(GRID_W)[:, None]
    kcol = jnp.arange(GRID_W)[None, :]
    c_start = jnp.clip(qcol - WIN_W // 2, 0, GRID_W - WIN_W)
    in_win = (kcol >= c_start) & (kcol < c_start + WIN_W)
    ring = 2 * GRID_W
    fill = jnp.full(rpb.shape[:2] + (ring - (2 * WIN_W - 1),), MASK_VALUE, rpb.dtype)
    ext = jnp.concatenate([rpb[:, :, WIN_W - 1:], fill, rpb[:, :, :WIN_W - 1]], axis=-1)
    toep = jnp.tile(ext, (1, 1, GRID_W))[:, :, :GRID_W * (ring - 1)]
    toep = toep.reshape(rpb.shape[:2] + (GRID_W, ring - 1))[..., :GRID_W]
    band = jnp.where(in_win[None, None], toep * LOG2_E, MASK_VALUE).astype(F32)
    n_dr = band.shape[1]
    band = band.reshape(NA_HEADS // 2, 2, n_dr, GRID_W, GRID_W).transpose(0, 2, 1, 3, 4)
    band = band.reshape(NA_HEADS // 2, n_dr, 2 * GRID_W, GRID_W)
    return jnp.concatenate([band[:, :-1], band[:, 1:]], axis=-1)


def _gla_kernel(*refs, reverse, has_init, emit_o, combine, emit_state):
    refs = list(refs)
    q_ref, k_ref, v_ref, lr_ref, w2_ref, b2_ref = refs[:6]
    pos = 6
    s0_ref = ob_ref = ng_ref = out_ref = sfin_ref = None
    if has_init:
        s0_ref = refs[pos]; pos += 1
    if combine:
        ob_ref, ng_ref = refs[pos], refs[pos + 1]; pos += 2
    if emit_o:
        out_ref = refs[pos]; pos += 1
    if emit_state:
        sfin_ref = refs[pos]; pos += 1
    s_ref, cum_buf, kend_buf, dec_buf, att_buf, qin_buf = refs[pos:pos + 6]

    j = pl.program_id(2)
    nblk = pl.num_programs(2)
    blk_len = GLA_BLOCK
    half = blk_len // 2
    c_len = GLA_CHUNK

    @pl.when(j == 0)
    def _():
        if has_init:
            s_ref[...] = s0_ref[0, 0]
        else:
            s_ref[...] = jnp.zeros_like(s_ref)

    ri = lax.broadcasted_iota(jnp.int32, (blk_len, blk_len), 0)
    ci = lax.broadcasted_iota(jnp.int32, (blk_len, blk_len), 1)
    tri = (ci >= ri) if reverse else (ci <= ri)
    tri_b = tri.astype(BF16)
    tri_h = tri[:half, :half]
    contract_last = (((1,), (1,)), ((), ()))
    contract_first = (((0,), (0,)), ((), ()))

    def bcast_halves(lo_row, hi_row):
        return jnp.concatenate([jnp.broadcast_to(lo_row, (half, lo_row.shape[1])),
                                jnp.broadcast_to(hi_row, (half, hi_row.shape[1]))], axis=0)

    def decay(r0):
        rows = slice(r0, r0 + blk_len)
        z = jnp.dot(lr_ref[0, rows, :].astype(BF16), w2_ref[0], preferred_element_type=F32) + b2_ref[0]
        g = (jnp.minimum(z, 0.0) - jnp.log(1.0 + jnp.exp(-jnp.abs(z)))) * (1.0 / GATE_NORMALIZER)
        g_hi = g.astype(BF16)
        g_lo = (g - g_hi.astype(F32)).astype(BF16)
        cum_buf[rows, :] = (jnp.dot(tri_b, g_hi, preferred_element_type=F32)
                            + jnp.dot(tri_b, g_lo, preferred_element_type=F32))

    def prepare(r0, slot):
        rows = slice(r0, r0 + blk_len)
        cum = cum_buf[rows, :]

        def row(i):
            return cum_buf[r0 + i:r0 + i + 1, :]

        if reverse:
            m_lo, m_hi, mid, end = row(c_len), row(half + c_len), row(half), row(0)
        else:
            m_lo, m_hi, mid, end = row(c_len - 1), row(half + c_len - 1), row(half - 1), row(blk_len - 1)
        m_ref = bcast_halves(m_lo, m_hi)
        def expb(a):
            return jnp.exp(a).astype(BF16)

        km = k_ref[0, rows, :] * expb(m_ref - cum)
        kend_buf[slot] = km * bcast_halves(expb(end - m_lo), expb(end - m_hi))
        dec_buf[slot] = jnp.exp(end)
        if emit_o:
            qm = q_ref[0, rows, :] * expb(cum - m_ref)
            a_lo = lax.dot_general(qm[:half], km[:half], contract_last, preferred_element_type=F32)
            a_hi = lax.dot_general(qm[half:], km[half:], contract_last, preferred_element_type=F32)
            a_lo = jnp.where(tri_h, a_lo, 0.0)
            a_hi = jnp.where(tri_h, a_hi, 0.0)
            zeros = jnp.zeros((half, half), F32)
            if reverse:
                a_off = lax.dot_general(qm[:half] * expb(m_lo - mid), km[half:] * expb(mid - m_hi),
                                        contract_last, preferred_element_type=F32)
                att = jnp.concatenate([jnp.concatenate([a_lo, a_off], axis=1),
                                       jnp.concatenate([zeros, a_hi], axis=1)], axis=0)
            else:
                a_off = lax.dot_general(qm[half:] * expb(m_hi - mid), km[:half] * expb(mid - m_lo),
                                        contract_last, preferred_element_type=F32)
                att = jnp.concatenate([jnp.concatenate([a_lo, zeros], axis=1),
                                       jnp.concatenate([a_off, a_hi], axis=1)], axis=0)
            att_buf[slot] = att.astype(BF16)
            qin_buf[slot] = qm * bcast_halves(expb(m_lo), expb(m_hi))

    def advance(r0, slot):
        rows = slice(r0, r0 + blk_len)
        v = v_ref[0, rows, :]
        st = s_ref[...]
        if emit_o:
            o = (jnp.dot(att_buf[slot], v, preferred_element_type=F32)
                 + lax.dot_general(qin_buf[slot], st.astype(BF16), contract_last, preferred_element_type=F32))
            if combine:
                tot = o + ob_ref[0, rows, :].astype(F32)
                ms = jnp.mean(tot * tot, axis=-1, keepdims=True)
                o = tot * lax.rsqrt(ms + RMS_EPS) * ng_ref[...]
            out_ref[0, rows, :] = o.astype(out_ref.dtype)
        s_ref[...] = st * dec_buf[slot] + lax.dot_general(v, kend_buf[slot], contract_first,
                                                          preferred_element_type=F32)

    nb = q_ref.shape[1] // blk_len
    order = list(range(nb - 1, -1, -1) if reverse else range(nb))
    for bi in order:
        decay(bi * blk_len)
    prepare(order[0] * blk_len, 0)
    for n, bi in enumerate(order):
        if n + 1 < nb:
            prepare(order[n + 1] * blk_len, (n + 1) % 2)
        advance(bi * blk_len, n % 2)

    if emit_state:
        @pl.when(j == nblk - 1)
        def _():
            sfin_ref[0, 0] = s_ref[...]


def _gla_scan(p_qk, p_v, g_lr, w2e, b2e, *, direction, tb, q_blk, k_blk, v_blk, dk, dv,
              s0=None, ob=None, norm_g=None, emit_o=True, emit_state=False):
    b, t, _ = p_qk.shape
    nblk = t // tb
    reverse = direction == 1
    combine = ob is not None
    has_init = s0 is not None

    def blk(j):
        return (nblk - 1 - j) if reverse else j

    in_specs = [pl.BlockSpec((1, tb, dk), lambda bi, h, j: (bi, blk(j), q_blk + h)),
                pl.BlockSpec((1, tb, dk), lambda bi, h, j: (bi, blk(j), k_blk + h)),
                pl.BlockSpec((1, tb, dv), lambda bi, h, j: (bi, blk(j), v_blk + h)),
                pl.BlockSpec((1, tb, LANES), lambda bi, h, j: (bi, blk(j), 0)),
                pl.BlockSpec((1, LANES, dk), lambda bi, h, j: (direction, 0, h)),
                pl.BlockSpec((1, 1, dk), lambda bi, h, j: (direction, 0, h))]
    args = [p_qk, p_qk, p_v, g_lr, w2e, b2e]
    if has_init:
        in_specs.append(pl.BlockSpec((1, 1, dv, dk), lambda bi, h, j: (bi, h, 0, 0)))
        args.append(s0)
    if combine:
        in_specs.append(pl.BlockSpec((1, tb, dv), lambda bi, h, j: (bi, blk(j), h)))
        in_specs.append(pl.BlockSpec((1, dv), lambda bi, h, j: (0, 0)))
        args += [ob, norm_g]
    out_specs, out_shape = [], []
    if emit_o:
        out_specs.append(pl.BlockSpec((1, tb, dv), lambda bi, h, j: (bi, blk(j), h)))
        out_shape.append(jax.ShapeDtypeStruct((b, t, GLA_HEADS * dv), BF16))
    if emit_state:
        out_specs.append(pl.BlockSpec((1, 1, dv, dk), lambda bi, h, j: (bi, h, 0, 0)))
        out_shape.append(jax.ShapeDtypeStruct((b, GLA_HEADS, dv, dk), F32))
    assert tb % GLA_BLOCK == 0
    kern = functools.partial(_gla_kernel, reverse=reverse, has_init=has_init,
                             emit_o=emit_o, combine=combine, emit_state=emit_state)
    res = pl.pallas_call(
        kern,
        grid=(b, GLA_HEADS, nblk),
        in_specs=in_specs,
        out_specs=out_specs,
        out_shape=out_shape,
        scratch_shapes=[pltpu.VMEM((dv, dk), F32),
                        pltpu.VMEM((tb, dk), F32),
                        pltpu.VMEM((2, GLA_BLOCK, dk), BF16),
                        pltpu.VMEM((2, 1, dk), F32),
                        pltpu.VMEM((2, GLA_BLOCK, GLA_BLOCK), BF16),
                        pltpu.VMEM((2, GLA_BLOCK, dk), BF16)],
        compiler_params=_params(("parallel", "parallel", "arbitrary")),
        name="gla_" + ("bwd" if reverse else "fwd") + ("_ctx" if not emit_o else ""),
    )(*args)
    return res[0] if len(res) == 1 else res


def _branch_merge_kernel(ya_ref, az_ref, yb_ref, bz_ref, ga_ref, gb_ref, wa_ref, wb_ref, m_ref, *, chunk):
    for r0 in range(0, ya_ref.shape[1], chunk):
        rows = slice(r0, r0 + chunk)
        ua = ya_ref[0, rows, :] * az_ref[0, rows, :]
        ub = yb_ref[0, rows, :] * bz_ref[0, rows, :]
        pa = jnp.dot(ua, wa_ref[...], preferred_element_type=F32)
        pb = jnp.dot(ub, wb_ref[...], preferred_element_type=F32)
        m_ref[0, rows, :] = (ga_ref[0, rows, :].astype(F32) * pa
                             + gb_ref[0, rows, :].astype(F32) * pb).astype(m_ref.dtype)


def _output_kernel(m_ref, x_ref, mod_ref, wo_ref, lng_ref, lnb_ref, o_ref, *, chunk):
    for r0 in range(0, m_ref.shape[1], chunk):
        rows = slice(r0, r0 + chunk)
        out = jnp.dot(m_ref[0, rows, :], wo_ref[...], preferred_element_type=F32)
        z = DEEPNORM_ALPHA * x_ref[0, rows, :] + mod_ref[0, 2:3, :] * out
        mu = jnp.mean(z, axis=-1, keepdims=True)
        zc = z - mu
        var = jnp.mean(zc * zc, axis=-1, keepdims=True)
        o_ref[0, rows, :] = zc * lax.rsqrt(var + LN_EPS) * lng_ref[...] + lnb_ref[...]


def _merge(y_a, y_b, p_silu, p_gate, x, mod, wa, wb, wo, ln_g, ln_b, az_blk, bz_blk, ga_blk, gb_blk,
           tm=MERGE_ROW_TILE, chunk=MERGE_ROW_CHUNK):
    b, s, d = x.shape
    na = y_a.shape[2]
    dv = y_b.shape[2]
    tm = min(tm, s)
    chunk = min(chunk, tm)
    assert s % tm == 0 and tm % chunk == 0
    const = dict(pipeline_mode=pl.Buffered(1))
    merged = pl.pallas_call(
        functools.partial(_branch_merge_kernel, chunk=chunk),
        grid=(b, s // tm),
        in_specs=[pl.BlockSpec((1, tm, na), lambda bi, i: (bi, i, 0)),
                  pl.BlockSpec((1, tm, na), lambda bi, i: (bi, i, az_blk)),
                  pl.BlockSpec((1, tm, dv), lambda bi, i: (bi, i, 0)),
                  pl.BlockSpec((1, tm, dv), lambda bi, i: (bi, i, bz_blk)),
                  pl.BlockSpec((1, tm, d), lambda bi, i: (bi, i, ga_blk)),
                  pl.BlockSpec((1, tm, d), lambda bi, i: (bi, i, gb_blk)),
                  pl.BlockSpec((na, d), lambda bi, i: (0, 0), **const),
                  pl.BlockSpec((dv, d), lambda bi, i: (0, 0), **const)],
        out_specs=pl.BlockSpec((1, tm, d), lambda bi, i: (bi, i, 0)),
        out_shape=jax.ShapeDtypeStruct((b, s, d), BF16),
        compiler_params=_params(("parallel", "parallel")),
        name="branch_merge",
    )(y_a, p_silu, y_b, p_silu, p_gate, p_gate, wa, wb)
    return pl.pallas_call(
        functools.partial(_output_kernel, chunk=chunk),
        grid=(b, s // tm),
        in_specs=[pl.BlockSpec((1, tm, d), lambda bi, i: (bi, i, 0)),
                  pl.BlockSpec((1, tm, d), lambda bi, i: (bi, i, 0)),
                  pl.BlockSpec((1, 3, d), lambda bi, i: (bi, 0, 0)),
                  pl.BlockSpec((d, d), lambda bi, i: (0, 0), **const),
                  pl.BlockSpec((1, d), lambda bi, i: (0, 0)),
                  pl.BlockSpec((1, d), lambda bi, i: (0, 0))],
        out_specs=pl.BlockSpec((1, tm, d), lambda bi, i: (bi, i, 0)),
        out_shape=jax.ShapeDtypeStruct((b, s, d), F32),
        compiler_params=_params(("parallel", "parallel")),
        name="output",
    )(merged, x, mod, wo, ln_g, ln_b)


def _rope_tables(s):
    pairs = LANES // 2
    inv_freq = ROPE_BASE ** (-jnp.arange(pairs, dtype=F32) / pairs)

    def table(n):
        ang = jnp.arange(n).astype(F32)[:, None] * inv_freq[None, :]
        return jnp.concatenate([jnp.cos(ang), jnp.cos(ang), -jnp.sin(ang), jnp.sin(ang)], axis=-1)

    return table(s // GRID_W), table(GRID_W)


def _identity_rope_tables(s):
    def table(n):
        return jnp.concatenate([jnp.ones((n, LANES), F32), jnp.zeros((n, LANES), F32)], axis=-1)

    return table(s // GRID_W), table(GRID_W)


def _expand_gate(w_gate2, b_gate, dk):
    dirs, rank, _ = w_gate2.shape
    pairs = dk // 4

    def dup(a):
        a = a.reshape(a.shape[:-1] + (GLA_HEADS, 2, pairs))
        a = jnp.concatenate([a, a], axis=-1)
        return a.reshape(a.shape[:-3] + (GLA_HEADS * dk,))

    w = dup(w_gate2.astype(F32))
    w2e = jnp.zeros((dirs, LANES, GLA_HEADS * dk), F32)
    for d in range(dirs):
        w2e = w2e.at[d, d * rank:(d + 1) * rank].set(w[d])
    b2e = dup(b_gate.astype(F32))[:, None, :]
    return w2e, b2e


def kernel(x, c, ctx, c_ctx, w_mod, b_mod, w_in, na_rpb, gla_w_gate2, gla_b_gate, gla_norm_g,
           w_br_a, w_br_b, w_out, ln_g, ln_b):
    bsz, s, d = x.shape
    lc = ctx.shape[1]
    key_dim, val_dim = d // 2, d
    dk, dv = key_dim // GLA_HEADS, val_dim // GLA_HEADS
    lyr = 0

    widths = (NA_WIDTH,) * 4 + (key_dim, key_dim, val_dim, val_dim, N_DIRS * GATE_RANK, d, d)
    offs = [0]
    for w_ in widths:
        offs.append(offs[-1] + w_)
    o_aq, o_ak, o_av, o_az, o_bq, o_bk, o_bv, o_bz, o_bg, o_ga, o_gb, o_end = offs
    tn = PROJ_COL_TILE
    tm = min(LN_ROW_TILE, s)
    assert all(o % tn == 0 for o in offs[:9]) and (o_end - o_ga) % tn == 0
    w_full = w_in[lyr]
    w_head = w_full.astype(BF16)
    w_sig = w_head[:, o_ga:o_end]
    w_gate1 = jnp.pad(w_head[:, o_bg:o_ga], ((0, 0), (0, LANES - (o_ga - o_bg))))

    def tiles(*ranges):
        return tuple(t for lo, hi in ranges for t in range(lo // tn, hi // tn))

    lin_tiles = tiles((o_aq, o_az), (o_bv, o_bz))
    rot_tiles = tiles((o_bq, o_bv))
    silu_tiles = tiles((o_bz, o_bg), (o_az, o_bq))
    sig_tiles = tiles((0, o_end - o_ga))

    cs = jnp.concatenate([c, c_ctx[None], jnp.zeros((MOD_ROWS - bsz - 1, d), F32)], axis=0)
    mod = _modulation(cs, w_mod[lyr], b_mod[lyr][None])
    mod_lat = mod[:bsz].reshape(bsz, 3, d)
    mod_ctx = mod[bsz].reshape(1, 3, d)

    na_scale = dict(scaled_tiles=NA_WIDTH // tn, scale=NA_HEAD_DIM ** -0.5 * LOG2_E)
    rot_scale = dict(scaled_tiles=key_dim // tn, scale=dk ** -0.5)
    p_lin, g_lr, h = _ln_projection(x, mod_lat, w_head, lin_tiles, w_gate1, tm, tn, **na_scale)
    tm_act = min(ACT_ROW_TILE, s)
    p_rot = _act_projection(h, w_head, rot_tiles, "rope", tm_act, tn, tables=_rope_tables(s), **rot_scale)
    p_silu = _act_projection(h, w_head, silu_tiles, "silu", tm_act, tn)
    p_sig = _act_projection(h, w_sig, sig_tiles, "sigmoid", tm_act, tn)
    ctx_rows = ctx.reshape(1, bsz * lc, d)
    p_lin_c, g_lr_c, h_c = _ln_projection(ctx_rows, mod_ctx, w_head, lin_tiles, w_gate1, bsz * lc, tn, **na_scale)
    p_rot_c = _act_projection(h_c, w_head, rot_tiles, "rope", bsz * lc, tn,
                              tables=_identity_rope_tables(bsz * lc), **rot_scale)
    p_lin_c, g_lr_c, p_rot_c = (a.reshape(bsz, lc, a.shape[-1]) for a in (p_lin_c, g_lr_c, p_rot_c))

    y_a = _neighbourhood_attention(p_lin, p_lin_c, _na_bias_table(na_rpb[lyr]))

    w2e, b2e = _expand_gate(gla_w_gate2[lyr], gla_b_gate[lyr], dk)
    w2e = w2e.astype(BF16)
    blocks = dict(q_blk=0, k_blk=key_dim // dk, v_blk=3 * NA_WIDTH // dv, dk=dk, dv=dv)
    s_f = _gla_scan(p_rot_c, p_lin_c, g_lr_c, w2e, b2e, direction=0, tb=lc, emit_o=False, emit_state=True,
                    **blocks)
    s_b = _gla_scan(p_rot_c, p_lin_c, g_lr_c, w2e, b2e, direction=1, tb=lc, emit_o=False, emit_state=True,
                    **blocks)
    tb = min(GLA_ROWS_PER_STEP, s)
    o_b = _gla_scan(p_rot, p_lin, g_lr, w2e, b2e, direction=1, tb=tb, s0=s_b, **blocks)
    y_b = _gla_scan(p_rot, p_lin, g_lr, w2e, b2e, direction=0, tb=tb, s0=s_f, ob=o_b,
                    norm_g=gla_norm_g[lyr][None].astype(F32), **blocks)

    return _merge(y_a, y_b, p_silu, p_sig, x, mod_lat,
                  w_br_a[lyr].astype(BF16), w_br_b[lyr].astype(BF16), w_out[lyr].astype(BF16),
                  ln_g[lyr][None], ln_b[lyr][None],
                  az_blk=val_dim // NA_WIDTH, bz_blk=0, ga_blk=0, gb_blk=1)
```

```python
import functools

import jax
import jax.numpy as jnp
from jax import lax
from jax.experimental import pallas as pl
from jax.experimental.pallas import tpu as pltpu

F32 = jnp.float32
BF16 = jnp.bfloat16

GRID_W = 64
WIN_H = 8
WIN_W = 16
NA_HEADS = 16
NA_HEAD_DIM = 64
NA_WIDTH = NA_HEADS * NA_HEAD_DIM
GLA_HEADS = 4
GATE_RANK = 16
GATE_NORMALIZER = 16.0
GLA_CHUNK = 64
GLA_BLOCK = 4 * GLA_CHUNK
N_DIRS = 2
ROPE_BASE = 10000.0
DEPTH = 1
DEEPNORM_ALPHA = (2 * DEPTH) ** 0.25
LN_EPS = 1e-6
RMS_EPS = 1e-6
MASK_VALUE = -1e30

LANES = 128
MOD_ROWS = 8
VMEM_LIMIT = 56 * 2 ** 20

MOD_COL_TILE = 768
PROJ_COL_TILE = 1024
LN_ROW_TILE = 1024
LN_ROW_CHUNK = 256
ACT_ROW_TILE = 2048
NA_ROWS_PER_STEP = 64
GLA_ROWS_PER_STEP = 4096
MERGE_ROW_TILE = 512
MERGE_ROW_CHUNK = 256


def _params(sem, vmem=VMEM_LIMIT):
    return pltpu.CompilerParams(dimension_semantics=sem, vmem_limit_bytes=vmem)


def _mod_kernel(c_ref, w_ref, b_ref, o_ref):
    c = c_ref[...]
    s = c * jax.nn.sigmoid(c)
    o_ref[...] = jnp.dot(s, w_ref[...], preferred_element_type=F32) + b_ref[...]


def _modulation(cs, w, b, tn=MOD_COL_TILE):
    rows, d = cs.shape
    n = w.shape[1]
    assert n % tn == 0
    return pl.pallas_call(
        _mod_kernel,
        grid=(n // tn,),
        in_specs=[pl.BlockSpec((rows, d), lambda j: (0, 0)),
                  pl.BlockSpec((d, tn), lambda j: (0, j)),
                  pl.BlockSpec((1, tn), lambda j: (0, j))],
        out_specs=pl.BlockSpec((rows, tn), lambda j: (0, j)),
        out_shape=jax.ShapeDtypeStruct((rows, n), F32),
        compiler_params=_params(("parallel",)),
        name="mod",
    )(cs, w, b)


def _ln_proj_kernel(x_ref, mod_ref, w_ref, wg_ref, p_ref, g_ref, h_ref, *, ln_rows, scaled_tiles, scale):
    n = pl.program_id(2)
    tm = x_ref.shape[1]
    col_scale = jnp.where(n < scaled_tiles, scale, 1.0)

    @pl.when(n == 0)
    def _():
        shift = mod_ref[0, 0:1, :]
        scale1 = 1.0 + mod_ref[0, 1:2, :]
        for r0 in range(0, tm, ln_rows):
            rows = slice(r0, r0 + ln_rows)
            xs = x_ref[0, rows, :]
            mu = jnp.mean(xs, axis=-1, keepdims=True)
            xc = xs - mu
            var = jnp.mean(xc * xc, axis=-1, keepdims=True)
            hb = (xc * lax.rsqrt(var + LN_EPS) * scale1 + shift).astype(BF16)
            h_ref[0, rows, :] = hb
            g_ref[0, rows, :] = jnp.dot(hb, wg_ref[...], preferred_element_type=F32)
            p_ref[0, rows, :] = (jnp.dot(hb, w_ref[...], preferred_element_type=F32) * col_scale).astype(p_ref.dtype)

    @pl.when(n > 0)
    def _():
        acc = jnp.dot(h_ref[0], w_ref[...], preferred_element_type=F32)
        p_ref[0] = (acc * col_scale).astype(p_ref.dtype)


def _column_tile_map(col_tiles):
    def index_map(bi, i, j):
        tile = col_tiles[0]
        for pos in range(1, len(col_tiles)):
            tile = jnp.where(j >= pos, col_tiles[pos], tile)
        return (0, tile)
    return index_map


def _ln_projection(xin, mod, w, col_tiles, wg, tm, tn, scaled_tiles, scale, ln_rows=LN_ROW_CHUNK):
    b, t, d = xin.shape
    n = len(col_tiles) * tn
    ln_rows = min(ln_rows, tm)
    assert t % tm == 0 and tm % ln_rows == 0
    kern = functools.partial(_ln_proj_kernel, ln_rows=ln_rows, scaled_tiles=scaled_tiles, scale=scale)
    return pl.pallas_call(
        kern,
        grid=(b, t // tm, n // tn),
        in_specs=[pl.BlockSpec((1, tm, d), lambda bi, i, j: (bi, i, 0)),
                  pl.BlockSpec((1, 3, d), lambda bi, i, j: (bi, 0, 0)),
                  pl.BlockSpec((d, tn), _column_tile_map(col_tiles)),
                  pl.BlockSpec((d, LANES), lambda bi, i, j: (0, 0))],
        out_specs=[pl.BlockSpec((1, tm, tn), lambda bi, i, j: (bi, i, j)),
                   pl.BlockSpec((1, tm, LANES), lambda bi, i, j: (bi, i, 0)),
                   pl.BlockSpec((1, tm, d), lambda bi, i, j: (bi, i, 0))],
        out_shape=[jax.ShapeDtypeStruct((b, t, n), BF16),
                   jax.ShapeDtypeStruct((b, t, LANES), F32),
                   jax.ShapeDtypeStruct((b, t, d), BF16)],
        compiler_params=_params(("parallel", "parallel", "arbitrary")),
        name="ln_proj",
    )(xin, mod, w, wg)


def _act_proj_kernel(*refs, kind, scaled_tiles, scale):
    if kind == "rope":
        h_ref, w_ref, rowtab_ref, coltab_ref, o_ref = refs
    else:
        h_ref, w_ref, o_ref = refs
    acc = jnp.dot(h_ref[0], w_ref[...], preferred_element_type=F32)
    if kind == "silu":
        half = 0.5 * acc
        o_ref[0] = (half * jnp.tanh(half) + half).astype(o_ref.dtype)
    elif kind == "sigmoid":
        o_ref[0] = (0.5 * jnp.tanh(0.5 * acc) + 0.5).astype(o_ref.dtype)
    else:
        acc = acc * jnp.where(pl.program_id(2) < scaled_tiles, scale, 1.0)
        for r in range(acc.shape[0] // GRID_W):
            rows = slice(r * GRID_W, (r + 1) * GRID_W)
            for j in range(acc.shape[1] // LANES):
                lanes = slice(j * LANES, (j + 1) * LANES)
                u = acc[rows, lanes]
                if j % 2 == 0:
                    c, s = rowtab_ref[r:r + 1, :LANES], rowtab_ref[r:r + 1, LANES:]
                else:
                    c, s = coltab_ref[:, :LANES], coltab_ref[:, LANES:]
                o_ref[0, rows, lanes] = (u * c + pltpu.roll(u, LANES // 2, 1) * s).astype(o_ref.dtype)


def _act_projection(h, w, col_tiles, kind, tm, tn, tables=(), scaled_tiles=0, scale=1.0):
    b, t, d = h.shape
    n = len(col_tiles) * tn
    kern = functools.partial(_act_proj_kernel, kind=kind, scaled_tiles=scaled_tiles, scale=scale)
    in_specs = [pl.BlockSpec((1, tm, d), lambda bi, i, j: (bi, i, 0)),
                pl.BlockSpec((d, tn), _column_tile_map(col_tiles))]
    if tables:
        in_specs += [pl.BlockSpec((tm // GRID_W, 2 * LANES), lambda bi, i, j: (i, 0)),
                     pl.BlockSpec((GRID_W, 2 * LANES), lambda bi, i, j: (0, 0))]
    return pl.pallas_call(
        kern,
        grid=(b, t // tm, n // tn),
        in_specs=in_specs,
        out_specs=pl.BlockSpec((1, tm, tn), lambda bi, i, j: (bi, i, j)),
        out_shape=jax.ShapeDtypeStruct((b, t, n), BF16),
        compiler_params=_params(("parallel", "parallel", "arbitrary")),
        name="proj_" + kind,
    )(h, w, *tables)


def _na_halo_rows(rb, rows):
    return min(rb + WIN_H, rows)


def _na_halo_start(blk, rb, rows):
    return jnp.clip(blk * rb - WIN_H // 2, 0, rows - _na_halo_rows(rb, rows))


def _na_kernel(q_ref, k_ref, v_ref, kc_ref, vc_ref, bias_ref, o_ref, s_buf, p_buf, l_buf, *, rb, rows):
    blk = pl.program_id(2)
    hd = NA_HEAD_DIM
    n_loc = WIN_H * GRID_W
    is_a = lax.broadcasted_iota(jnp.int32, (GRID_W, 2 * hd), 1) < hd
    contract_last = (((1,), (1,)), ((), ()))

    halo_start = _na_halo_start(blk, rb, rows)

    def window_start(i):
        r = blk * rb + i
        r_start = jnp.clip(r - WIN_H // 2, 0, rows - WIN_H)
        return r_start - r + (WIN_H - 1), pl.multiple_of((r_start - halo_start) * GRID_W, GRID_W)

    def scores(i):
        cls, k0 = window_start(i)
        q = q_ref[0, i * GRID_W:(i + 1) * GRID_W, :]
        zero = jnp.zeros_like(q)
        qs = jnp.concatenate([jnp.where(is_a, q, zero), jnp.where(is_a, zero, q)], axis=0)
        k8 = k_ref[0, pl.ds(k0, n_loc), :]
        bias = jnp.concatenate([bias_ref[0, cls + 2 * jj] for jj in range(WIN_H // 2)], axis=1)
        s_buf[i % 2, :, :n_loc] = lax.dot_general(qs, k8, contract_last, preferred_element_type=F32) + bias
        s_buf[i % 2, :, n_loc:] = lax.dot_general(qs, kc_ref[0], contract_last, preferred_element_type=F32)

    def softmax(i):
        s = s_buf[i % 2]
        m = jnp.max(s, axis=-1, keepdims=True)
        p = jnp.exp(s - m)
        l_buf[i % 2] = jnp.sum(p, axis=-1, keepdims=True)
        p_buf[i % 2] = p.astype(BF16)

    def values(i):
        _, k0 = window_start(i)
        v8 = v_ref[0, pl.ds(k0, n_loc), :]
        o = (jnp.dot(p_buf[i % 2, :, :n_loc], v8, preferred_element_type=F32)
             + jnp.dot(p_buf[i % 2, :, n_loc:], vc_ref[0], preferred_element_type=F32))
        o = o / l_buf[i % 2]
        out = jnp.where(is_a, o[:GRID_W], o[GRID_W:])
        o_ref[0, i * GRID_W:(i + 1) * GRID_W, :] = out.astype(o_ref.dtype)

    for t in range(rb + 2):
        if t >= 2:
            values(t - 2)
        if 1 <= t <= rb:
            softmax(t - 1)
        if t < rb:
            scores(t)


def _neighbourhood_attention(p, p_c, bias, rb=NA_ROWS_PER_STEP):
    b, s, _ = p.shape
    lc = p_c.shape[1]
    rows = s // GRID_W
    rb = min(rb, rows)
    assert s % GRID_W == 0 and rows % rb == 0 and rows >= WIN_H
    hg = NA_WIDTH // LANES
    kern = functools.partial(_na_kernel, rb=rb, rows=rows)
    n_keys = WIN_H * GRID_W + lc
    halo_tokens = _na_halo_rows(rb, rows) * GRID_W

    def halo_spec(first_group):
        return pl.BlockSpec(
            (pl.Element(1), pl.Element(halo_tokens), pl.Element(LANES)),
            lambda bi, g, i: (bi, _na_halo_start(i, rb, rows) * GRID_W, (first_group + g) * LANES))

    return pl.pallas_call(
        kern,
        grid=(b, hg, rows // rb),
        in_specs=[pl.BlockSpec((1, rb * GRID_W, LANES), lambda bi, g, i: (bi, i, g)),
                  halo_spec(hg), halo_spec(2 * hg),
                  pl.BlockSpec((1, lc, LANES), lambda bi, g, i: (bi, 0, hg + g)),
                  pl.BlockSpec((1, lc, LANES), lambda bi, g, i: (bi, 0, 2 * hg + g)),
                  pl.BlockSpec((1, 2 * WIN_H - 2, 2 * GRID_W, 2 * GRID_W), lambda bi, g, i: (g, 0, 0, 0))],
        out_specs=pl.BlockSpec((1, rb * GRID_W, LANES), lambda bi, g, i: (bi, i, g)),
        out_shape=jax.ShapeDtypeStruct((b, s, NA_WIDTH), BF16),
        scratch_shapes=[pltpu.VMEM((2, 2 * GRID_W, n_keys), F32),
                        pltpu.VMEM((2, 2 * GRID_W, n_keys), BF16),
                        pltpu.VMEM((2, 2 * GRID_W, 1), F32)],
        compiler_params=_params(("parallel", "parallel", "arbitrary")),
        name="na",
    )(p, p, p, p_c, p_c, bias)


def _na_bias_table(rpb):
    qcol = jnp.arange(GRID_W)[:, None]
    kcol = jnp.arange(GRID_W)[None, :]
    c_start = jnp.clip(qcol - WIN_W // 2, 0, GRID_W - WIN_W)
    in_win = (kcol >= c_start) & (kcol < c_start + WIN_W)
    ring = 2 * GRID_W
    fill = jnp.full(rpb.shape[:2] + (ring - (2 * WIN_W - 1),), MASK_VALUE, rpb.dtype)
    ext = jnp.concatenate([rpb[:, :, WIN_W - 1:], fill, rpb[:, :, :WIN_W - 1]], axis=-1)
    toep = jnp.tile(ext, (1, 1, GRID_W))[:, :, :GRID_W * (ring - 1)]
    toep = toep.reshape(rpb.shape[:2] + (GRID_W, ring - 1))[..., :GRID_W]
    band = jnp.where(in_win[None, None], toep, MASK_VALUE).astype(F32)
    n_dr = band.shape[1]
    band = band.reshape(NA_HEADS // 2, 2, n_dr, GRID_W, GRID_W).transpose(0, 2, 1, 3, 4)
    band = band.reshape(NA_HEADS // 2, n_dr, 2 * GRID_W, GRID_W)
    return jnp.concatenate([band[:, :-1], band[:, 1:]], axis=-1)


def _gla_kernel(*refs, reverse, has_init, emit_o, combine, emit_state):
    refs = list(refs)
    q_ref, k_ref, v_ref, lr_ref, w2_ref, b2_ref = refs[:6]
    pos = 6
    s0_ref = ob_ref = ng_ref = out_ref = sfin_ref = None
    if has_init:
        s0_ref = refs[pos]; pos += 1
    if combine:
        ob_ref, ng_ref = refs[pos], refs[pos + 1]; pos += 2
    if emit_o:
        out_ref = refs[pos]; pos += 1
    if emit_state:
        sfin_ref = refs[pos]; pos += 1
    s_ref, cum_buf, kend_buf, dec_buf, att_buf, qin_buf = refs[pos:pos + 6]

    j = pl.program_id(2)
    nblk = pl.num_programs(2)
    blk_len = GLA_BLOCK
    half = blk_len // 2
    c_len = GLA_CHUNK

    @pl.when(j == 0)
    def _():
        if has_init:
            s_ref[...] = s0_ref[0, 0]
        else:
            s_ref[...] = jnp.zeros_like(s_ref)

    ri = lax.broadcasted_iota(jnp.int32, (blk_len, blk_len), 0)
    ci = lax.broadcasted_iota(jnp.int32, (blk_len, blk_len), 1)
    tri = (ci >= ri) if reverse else (ci <= ri)
    tri_b = tri.astype(BF16)
    tri_h = tri[:half, :half]
    contract_last = (((1,), (1,)), ((), ()))
    contract_first = (((0,), (0,)), ((), ()))

    def bcast_halves(lo_row, hi_row):
        return jnp.concatenate([jnp.broadcast_to(lo_row, (half, lo_row.shape[1])),
                                jnp.broadcast_to(hi_row, (half, hi_row.shape[1]))], axis=0)

    def decay(r0):
        rows = slice(r0, r0 + blk_len)
        z = jnp.dot(lr_ref[0, rows, :].astype(BF16), w2_ref[0], preferred_element_type=F32) + b2_ref[0]
        g = (jnp.minimum(z, 0.0) - jnp.log(1.0 + jnp.exp(-jnp.abs(z)))) * (1.0 / GATE_NORMALIZER)
        cum_buf[rows, :] = jnp.dot(tri_b, g.astype(BF16), preferred_element_type=F32)

    def prepare(r0, slot):
        rows = slice(r0, r0 + blk_len)
        cum = cum_buf[rows, :]

        def row(i):
            return cum_buf[r0 + i:r0 + i + 1, :]

        if reverse:
            m_lo, m_hi, mid, end = row(c_len), row(half + c_len), row(half), row(0)
        else:
            m_lo, m_hi, mid, end = row(c_len - 1), row(half + c_len - 1), row(half - 1), row(blk_len - 1)
        m_ref = bcast_halves(m_lo, m_hi)
        def expb(a):
            return jnp.exp(a).astype(BF16)

        km = k_ref[0, rows, :] * expb(m_ref - cum)
        kend_buf[slot] = km * bcast_halves(expb(end - m_lo), expb(end - m_hi))
        dec_buf[slot] = jnp.exp(end)
        if emit_o:
            qm = q_ref[0, rows, :] * expb(cum - m_ref)
            a_lo = lax.dot_general(qm[:half], km[:half], contract_last, preferred_element_type=F32)
            a_hi = lax.dot_general(qm[half:], km[half:], contract_last, preferred_element_type=F32)
            a_lo = jnp.where(tri_h, a_lo, 0.0)
            a_hi = jnp.where(tri_h, a_hi, 0.0)
            zeros = jnp.zeros((half, half), F32)
            if reverse:
                a_off = lax.dot_general(qm[:half] * expb(m_lo - mid), km[half:] * expb(mid - m_hi),
                                        contract_last, preferred_element_type=F32)
                att = jnp.concatenate([jnp.concatenate([a_lo, a_off], axis=1),
                                       jnp.concatenate([zeros, a_hi], axis=1)], axis=0)
            else:
                a_off = lax.dot_general(qm[half:] * expb(m_hi - mid), km[:half] * expb(mid - m_lo),
                                        contract_last, preferred_element_type=F32)
                att = jnp.concatenate([jnp.concatenate([a_lo, zeros], axis=1),
                                       jnp.concatenate([a_off, a_hi], axis=1)], axis=0)
            att_buf[slot] = att.astype(BF16)
            qin_buf[slot] = qm * bcast_halves(expb(m_lo), expb(m_hi))

    def advance(r0, slot):
        rows = slice(r0, r0 + blk_len)
        v = v_ref[0, rows, :]
        st = s_ref[...]
        if emit_o:
            o = (jnp.dot(att_buf[slot], v, preferred_element_type=F32)
                 + lax.dot_general(qin_buf[slot], st.astype(BF16), contract_last, preferred_element_type=F32))
            if combine:
                tot = o + ob_ref[0, rows, :].astype(F32)
                ms = jnp.mean(tot * tot, axis=-1, keepdims=True)
                o = tot * lax.rsqrt(ms + RMS_EPS) * ng_ref[...]
            out_ref[0, rows, :] = o.astype(out_ref.dtype)
        s_ref[...] = st * dec_buf[slot] + lax.dot_general(v, kend_buf[slot], contract_first,
                                                          preferred_element_type=F32)

    nb = q_ref.shape[1] // blk_len
    order = list(range(nb - 1, -1, -1) if reverse else range(nb))
    for bi in order:
        decay(bi * blk_len)
    prepare(order[0] * blk_len, 0)
    for n, bi in enumerate(order):
        if n + 1 < nb:
            prepare(order[n + 1] * blk_len, (n + 1) % 2)
        advance(bi * blk_len, n % 2)

    if emit_state:
        @pl.when(j == nblk - 1)
        def _():
            sfin_ref[0, 0] = s_ref[...]


def _gla_scan(p_qk, p_v, g_lr, w2e, b2e, *, direction, tb, q_blk, k_blk, v_blk, dk, dv,
              s0=None, ob=None, norm_g=None, emit_o=True, emit_state=False):
    b, t, _ = p_qk.shape
    nblk = t // tb
    reverse = direction == 1
    combine = ob is not None
    has_init = s0 is not None

    def blk(j):
        return (nblk - 1 - j) if reverse else j

    in_specs = [pl.BlockSpec((1, tb, dk), lambda bi, h, j: (bi, blk(j), q_blk + h)),
                pl.BlockSpec((1, tb, dk), lambda bi, h, j: (bi, blk(j), k_blk + h)),
                pl.BlockSpec((1, tb, dv), lambda bi, h, j: (bi, blk(j), v_blk + h)),
                pl.BlockSpec((1, tb, LANES), lambda bi, h, j: (bi, blk(j), 0)),
                pl.BlockSpec((1, LANES, dk), lambda bi, h, j: (direction, 0, h)),
                pl.BlockSpec((1, 1, dk), lambda bi, h, j: (direction, 0, h))]
    args = [p_qk, p_qk, p_v, g_lr, w2e, b2e]
    if has_init:
        in_specs.append(pl.BlockSpec((1, 1, dv, dk), lambda bi, h, j: (bi, h, 0, 0)))
        args.append(s0)
    if combine:
        in_specs.append(pl.BlockSpec((1, tb, dv), lambda bi, h, j: (bi, blk(j), h)))
        in_specs.append(pl.BlockSpec((1, dv), lambda bi, h, j: (0, 0)))
        args += [ob, norm_g]
    out_specs, out_shape = [], []
    if emit_o:
        out_specs.append(pl.BlockSpec((1, tb, dv), lambda bi, h, j: (bi, blk(j), h)))
        out_shape.append(jax.ShapeDtypeStruct((b, t, GLA_HEADS * dv), BF16))
    if emit_state:
        out_specs.append(pl.BlockSpec((1, 1, dv, dk), lambda bi, h, j: (bi, h, 0, 0)))
        out_shape.append(jax.ShapeDtypeStruct((b, GLA_HEADS, dv, dk), F32))
    assert tb % GLA_BLOCK == 0
    kern = functools.partial(_gla_kernel, reverse=reverse, has_init=has_init,
                             emit_o=emit_o, combine=combine, emit_state=emit_state)
    res = pl.pallas_call(
        kern,
        grid=(b, GLA_HEADS, nblk),
        in_specs=in_specs,
        out_specs=out_specs,
        out_shape=out_shape,
        scratch_shapes=[pltpu.VMEM((dv, dk), F32),
                        pltpu.VMEM((tb, dk), F32),
                        pltpu.VMEM((2, GLA_BLOCK, dk), BF16),
                        pltpu.VMEM((2, 1, dk), F32),
                        pltpu.VMEM((2, GLA_BLOCK, GLA_BLOCK), BF16),
                        pltpu.VMEM((2, GLA_BLOCK, dk), BF16)],
        compiler_params=_params(("parallel", "parallel", "arbitrary")),
        name="gla_" + ("bwd" if reverse else "fwd") + ("_ctx" if not emit_o else ""),
    )(*args)
    return res[0] if len(res) == 1 else res


def _branch_merge_kernel(ya_ref, az_ref, yb_ref, bz_ref, ga_ref, gb_ref, wa_ref, wb_ref, m_ref, *, chunk):
    for r0 in range(0, ya_ref.shape[1], chunk):
        rows = slice(r0, r0 + chunk)
        ua = ya_ref[0, rows, :] * az_ref[0, rows, :]
        ub = yb_ref[0, rows, :] * bz_ref[0, rows, :]
        pa = jnp.dot(ua, wa_ref[...], preferred_element_type=F32)
        pb = jnp.dot(ub, wb_ref[...], preferred_element_type=F32)
        m_ref[0, rows, :] = (ga_ref[0, rows, :].astype(F32) * pa
                             + gb_ref[0, rows, :].astype(F32) * pb).astype(m_ref.dtype)


def _output_kernel(m_ref, x_ref, mod_ref, wo_ref, lng_ref, lnb_ref, o_ref, *, chunk):
    for r0 in range(0, m_ref.shape[1], chunk):
        rows = slice(r0, r0 + chunk)
        out = jnp.dot(m_ref[0, rows, :], wo_ref[...], preferred_element_type=F32)
        z = DEEPNORM_ALPHA * x_ref[0, rows, :] + mod_ref[0, 2:3, :] * out
        mu = jnp.mean(z, axis=-1, keepdims=True)
        zc = z - mu
        var = jnp.mean(zc * zc, axis=-1, keepdims=True)
        o_ref[0, rows, :] = zc * lax.rsqrt(var + LN_EPS) * lng_ref[...] + lnb_ref[...]


def _merge(y_a, y_b, p_silu, p_gate, x, mod, wa, wb, wo, ln_g, ln_b, az_blk, bz_blk, ga_blk, gb_blk,
           tm=MERGE_ROW_TILE, chunk=MERGE_ROW_CHUNK):
    b, s, d = x.shape
    na = y_a.shape[2]
    dv = y_b.shape[2]
    tm = min(tm, s)
    chunk = min(chunk, tm)
    assert s % tm == 0 and tm % chunk == 0
    const = dict(pipeline_mode=pl.Buffered(1))
    merged = pl.pallas_call(
        functools.partial(_branch_merge_kernel, chunk=chunk),
        grid=(b, s // tm),
        in_specs=[pl.BlockSpec((1, tm, na), lambda bi, i: (bi, i, 0)),
                  pl.BlockSpec((1, tm, na), lambda bi, i: (bi, i, az_blk)),
                  pl.BlockSpec((1, tm, dv), lambda bi, i: (bi, i, 0)),
                  pl.BlockSpec((1, tm, dv), lambda bi, i: (bi, i, bz_blk)),
                  pl.BlockSpec((1, tm, d), lambda bi, i: (bi, i, ga_blk)),
                  pl.BlockSpec((1, tm, d), lambda bi, i: (bi, i, gb_blk)),
                  pl.BlockSpec((na, d), lambda bi, i: (0, 0), **const),
                  pl.BlockSpec((dv, d), lambda bi, i: (0, 0), **const)],
        out_specs=pl.BlockSpec((1, tm, d), lambda bi, i: (bi, i, 0)),
        out_shape=jax.ShapeDtypeStruct((b, s, d), BF16),
        compiler_params=_params(("parallel", "parallel")),
        name="branch_merge",
    )(y_a, p_silu, y_b, p_silu, p_gate, p_gate, wa, wb)
    return pl.pallas_call(
        functools.partial(_output_kernel, chunk=chunk),
        grid=(b, s // tm),
        in_specs=[pl.BlockSpec((1, tm, d), lambda bi, i: (bi, i, 0)),
                  pl.BlockSpec((1, tm, d), lambda bi, i: (bi, i, 0)),
                  pl.BlockSpec((1, 3, d), lambda bi, i: (bi, 0, 0)),
                  pl.BlockSpec((d, d), lambda bi, i: (0, 0), **const),
                  pl.BlockSpec((1, d), lambda bi, i: (0, 0)),
                  pl.BlockSpec((1, d), lambda bi, i: (0, 0))],
        out_specs=pl.BlockSpec((1, tm, d), lambda bi, i: (bi, i, 0)),
        out_shape=jax.ShapeDtypeStruct((b, s, d), F32),
        compiler_params=_params(("parallel", "parallel")),
        name="output",
    )(merged, x, mod, wo, ln_g, ln_b)


def _rope_tables(s):
    pairs = LANES // 2
    inv_freq = ROPE_BASE ** (-jnp.arange(pairs, dtype=F32) / pairs)

    def table(n):
        ang = jnp.arange(n).astype(F32)[:, None] * inv_freq[None, :]
        return jnp.concatenate([jnp.cos(ang), jnp.cos(ang), -jnp.sin(ang), jnp.sin(ang)], axis=-1)

    return table(s // GRID_W), table(GRID_W)


def _identity_rope_tables(s):
    def table(n):
        return jnp.concatenate([jnp.ones((n, LANES), F32), jnp.zeros((n, LANES), F32)], axis=-1)

    return table(s // GRID_W), table(GRID_W)


def _expand_gate(w_gate2, b_gate, dk):
    dirs, rank, _ = w_gate2.shape
    pairs = dk // 4

    def dup(a):
        a = a.reshape(a.shape[:-1] + (GLA_HEADS, 2, pairs))
        a = jnp.concatenate([a, a], axis=-1)
        return a.reshape(a.shape[:-3] + (GLA_HEADS * dk,))

    w = dup(w_gate2.astype(F32))
    w2e = jnp.zeros((dirs, LANES, GLA_HEADS * dk), F32)
    for d in range(dirs):
        w2e = w2e.at[d, d * rank:(d + 1) * rank].set(w[d])
    b2e = dup(b_gate.astype(F32))[:, None, :]
    return w2e, b2e


def kernel(x, c, ctx, c_ctx, w_mod, b_mod, w_in, na_rpb, gla_w_gate2, gla_b_gate, gla_norm_g,
           w_br_a, w_br_b, w_out, ln_g, ln_b):
    bsz, s, d = x.shape
    lc = ctx.shape[1]
    key_dim, val_dim = d // 2, d
    dk, dv = key_dim // GLA_HEADS, val_dim // GLA_HEADS
    lyr = 0

    widths = (NA_WIDTH,) * 4 + (key_dim, key_dim, val_dim, val_dim, N_DIRS * GATE_RANK, d, d)
    offs = [0]
    for w_ in widths:
        offs.append(offs[-1] + w_)
    o_aq, o_ak, o_av, o_az, o_bq, o_bk, o_bv, o_bz, o_bg, o_ga, o_gb, o_end = offs
    tn = PROJ_COL_TILE
    tm = min(LN_ROW_TILE, s)
    assert all(o % tn == 0 for o in offs[:9]) and (o_end - o_ga) % tn == 0
    w_full = w_in[lyr]
    w_head = w_full.astype(BF16)
    w_sig = w_head[:, o_ga:o_end]
    w_gate1 = jnp.pad(w_head[:, o_bg:o_ga], ((0, 0), (0, LANES - (o_ga - o_bg))))

    def tiles(*ranges):
        return tuple(t for lo, hi in ranges for t in range(lo // tn, hi // tn))

    lin_tiles = tiles((o_aq, o_az), (o_bv, o_bz))
    rot_tiles = tiles((o_bq, o_bv))
    silu_tiles = tiles((o_bz, o_bg), (o_az, o_bq))
    sig_tiles = tiles((0, o_end - o_ga))

    cs = jnp.concatenate([c, c_ctx[None], jnp.zeros((MOD_ROWS - bsz - 1, d), F32)], axis=0)
    mod = _modulation(cs, w_mod[lyr], b_mod[lyr][None])
    mod_lat = mod[:bsz].reshape(bsz, 3, d)
    mod_ctx = mod[bsz].reshape(1, 3, d)

    na_scale = dict(scaled_tiles=NA_WIDTH // tn, scale=NA_HEAD_DIM ** -0.5)
    rot_scale = dict(scaled_tiles=key_dim // tn, scale=dk ** -0.5)
    p_lin, g_lr, h = _ln_projection(x, mod_lat, w_head, lin_tiles, w_gate1, tm, tn, **na_scale)
    tm_act = min(ACT_ROW_TILE, s)
    p_rot = _act_projection(h, w_head, rot_tiles, "rope", tm_act, tn, tables=_rope_tables(s), **rot_scale)
    p_silu = _act_projection(h, w_head, silu_tiles, "silu", tm_act, tn)
    p_sig = _act_projection(h, w_sig, sig_tiles, "sigmoid", tm_act, tn)
    ctx_rows = ctx.reshape(1, bsz * lc, d)
    p_lin_c, g_lr_c, h_c = _ln_projection(ctx_rows, mod_ctx, w_head, lin_tiles, w_gate1, bsz * lc, tn, **na_scale)
    p_rot_c = _act_projection(h_c, w_head, rot_tiles, "rope", bsz * lc, tn,
                              tables=_identity_rope_tables(bsz * lc), **rot_scale)
    p_lin_c, g_lr_c, p_rot_c = (a.reshape(bsz, lc, a.shape[-1]) for a in (p_lin_c, g_lr_c, p_rot_c))

    y_a = _neighbourhood_attention(p_lin, p_lin_c, _na_bias_table(na_rpb[lyr]))

    w2e, b2e = _expand_gate(gla_w_gate2[lyr], gla_b_gate[lyr], dk)
    w2e = w2e.astype(BF16)
    blocks = dict(q_blk=0, k_blk=key_dim // dk, v_blk=3 * NA_WIDTH // dv, dk=dk, dv=dv)
    s_f = _gla_scan(p_rot_c, p_lin_c, g_lr_c, w2e, b2e, direction=0, tb=lc, emit_o=False, emit_state=True,
                    **blocks)
    s_b = _gla_scan(p_rot_c, p_lin_c, g_lr_c, w2e, b2e, direction=1, tb=lc, emit_o=False, emit_state=True,
                    **blocks)
    tb = min(GLA_ROWS_PER_STEP, s)
    o_b = _gla_scan(p_rot, p_lin, g_lr, w2e, b2e, direction=1, tb=tb, s0=s_b, **blocks)
    y_b = _gla_scan(p_rot, p_lin, g_lr, w2e, b2e, direction=0, tb=tb, s0=s_f, ob=o_b,
                    norm_g=gla_norm_g[lyr][None].astype(F32), **blocks)

    return _merge(y_a, y_b, p_silu, p_sig, x, mod_lat,
                  w_br_a[lyr].astype(BF16), w_br_b[lyr].astype(BF16), w_out[lyr].astype(BF16),
                  ln_g[lyr][None], ln_b[lyr][None],
                  az_blk=val_dim // NA_WIDTH, bz_blk=0, ga_blk=0, gb_blk=1)
```
